```python
import jax, jax.numpy as jnp
from jax import lax
import numpy as np

D_MODEL = 1024
BATCH = 8
SEQ = 4096
DEPTH = 1
DEC_BATCH = 32
DEC_SEQ = 4
PAST_LEN = 16384
PAGE_SIZE = 128

A_HEADS = 8
A_HD = 64
A_WIDTH = A_HEADS * A_HD
DILATED_BRANCHES = ((128, 1), (512, 4), (2048, 16))
A_MAX_WINDOW = 2048
B_HEADS = 4
B_DK = 128
B_DV = 128
B_WIDTH = B_HEADS * B_DV
B_CHUNK = 64
MIX_IN = 3 * A_WIDTH + 2 * B_HEADS * B_DK + 2 * B_WIDTH
MIX_WIDTH = A_WIDTH + B_WIDTH
N_MEM = 256
X_HEADS = 4
X_HD = 128
X_WIDTH = X_HEADS * X_HD
PEER_HEADS = 8
PEER_NKEYS = 128
PEER_N = PEER_NKEYS * PEER_NKEYS
PEER_QDIM = 256
PEER_HALF = PEER_QDIM // 2
PEER_TOPK = 16
PEER_BLOCK = 256
EPS = 1e-6

kernel_name = 'hymba_dilated_hgrn2_peer_step'


def rms_norm(x, g):
    xf = x.astype(jnp.float32)
    y = xf * lax.rsqrt(jnp.mean(xf * xf, axis=-1, keepdims=True) + EPS)
    return (y * g.astype(jnp.float32)).astype(x.dtype)


def _band_attention(q, k, v, span):
    N, n, H, hd = q.shape
    blk = span
    nb = -(-n // blk)
    pad = nb * blk - n

    def blocks(a):
        return jnp.pad(a, ((0, 0), (0, pad), (0, 0), (0, 0))).reshape(N, nb, blk, H, hd)

    def with_prev(a):
        prev = jnp.concatenate([jnp.zeros_like(a[:, :1]), a[:, :-1]], axis=1)
        return jnp.concatenate([prev, a], axis=2)

    qb = blocks(q)
    kk = with_prev(blocks(k))
    vv = with_prev(blocks(v))
    s = jnp.einsum('nbqhd,nbkhd->nbhqk', qb, kk, preferred_element_type=jnp.float32) * (hd ** -0.5)
    qi = np.arange(blk)[:, None]
    ki = np.arange(2 * blk)[None, :]
    dist = blk + qi - ki
    keypos = (np.arange(nb)[:, None, None] - 1) * blk + ki[None]
    mask = (dist >= 0) & (dist <= span) & (keypos >= 0)
    s = jnp.where(mask[None, :, None], s, -jnp.inf)
    lse = jax.nn.logsumexp(s, axis=-1)
    p = jnp.exp(s - lse[..., None]).astype(v.dtype)
    o = jnp.einsum('nbhqk,nbkhd->nbqhd', p, vv).reshape(N, nb * blk, H, hd)[:, :n]
    lse = lse.transpose(0, 1, 3, 2).reshape(N, nb * blk, H)[:, :n]
    return o, lse


def _combine_branches(outs, lses):
    w = jax.nn.softmax(jnp.stack(lses, axis=0), axis=0)
    return jnp.sum(w[..., None] * jnp.stack(outs, axis=0).astype(jnp.float32), axis=0)


def dilated_attention_prompt(q, k, v):
    B, L, H, hd = q.shape
    outs, lses = [], []
    for window, dil in DILATED_BRANCHES:
        n = L // dil

        def by_residue(a):
            return a.reshape(B, n, dil, H, hd).transpose(0, 2, 1, 3, 4).reshape(B * dil, n, H, hd)

        o, lse = _band_attention(by_residue(q), by_residue(k), by_residue(v), window // dil)
        outs.append(o.reshape(B, dil, n, H, hd).transpose(0, 2, 1, 3, 4).reshape(B, L, H, hd))
        lses.append(lse.reshape(B, dil, n, H).transpose(0, 2, 1, 3).reshape(B, L, H))
    return _combine_branches(outs, lses)


def dilated_attention_sample(q, k_new, v_new, k_buf, v_buf):
    W = k_buf.shape[1]
    T = q.shape[1]
    hd = q.shape[-1]
    k_all = jnp.concatenate([k_buf, k_new], axis=1)
    v_all = jnp.concatenate([v_buf, v_new], axis=1)
    outs, lses = [], []
    for window, dil in DILATED_BRANCHES:
        span = window // dil
        idx = (W + np.arange(T))[:, None] - dil * np.arange(span + 1)[None, :]
        valid = idx >= 0
        idx = np.maximum(idx, 0)
        kg = k_all[:, idx]
        vg = v_all[:, idx]
        s = jnp.einsum('bthd,btshd->bhts', q, kg, preferred_element_type=jnp.float32) * (hd ** -0.5)
        s = jnp.where(valid[None, None], s, -jnp.inf)
        lse = jax.nn.logsumexp(s, axis=-1)
        p = jnp.exp(s - lse[..., None]).astype(v_all.dtype)
        outs.append(jnp.einsum('bhts,btshd->bthd', p, vg))
        lses.append(lse.transpose(0, 2, 1))
    return _combine_branches(outs, lses)


def hgrn2_chunked(q, logf, k, i, s0):
    B, L, H, DK = q.shape
    C = min(B_CHUNK, L)
    nc = -(-L // C)
    pad = nc * C - L

    def chunks(a):
        a = jnp.pad(a, ((0, 0), (0, pad), (0, 0), (0, 0)))
        return jnp.moveaxis(a.reshape(B, nc, C, H, a.shape[-1]), 1, 0)

    tril = np.tril(np.ones((C, C), dtype=bool))

    def step(S, inp):
        qc, lfc, kc, ic = inp
        c = jnp.cumsum(lfc, axis=1)
        o_inter = jnp.einsum('bchk,bhkv->bchv', qc * jnp.exp(c), S)
        diff = c[:, :, None] - c[:, None, :]
        decay = jnp.exp(jnp.where(tril[None, :, :, None, None], diff, -jnp.inf))
        A = jnp.einsum('btshk,bshk->bhts', qc[:, :, None] * decay, kc)
        o_intra = jnp.einsum('bhts,bshv->bthv', A, ic)
        c_last = c[:, -1]
        S_new = jnp.exp(c_last)[..., None] * S + jnp.einsum(
            'bshk,bshv->bhkv', kc * jnp.exp(c_last[:, None] - c), ic)
        return S_new, o_inter + o_intra

    S_fin, o = lax.scan(step, s0.astype(jnp.float32), (chunks(q), chunks(logf), chunks(k), chunks(i)))
    o = jnp.moveaxis(o, 0, 1).reshape(B, nc * C, H, i.shape[-1])[:, :L]
    return o, S_fin


def _mix_proj(h, w_in, lb):
    B, L, _ = h.shape
    splits = np.cumsum([A_WIDTH, A_WIDTH, A_WIDTH, B_HEADS * B_DK, B_HEADS * B_DK, B_WIDTH])
    qa, ka, va, qb, fb, ib, gb = jnp.split(h @ w_in, splits, axis=-1)

    def heads_a(t):
        return t.reshape(B, L, A_HEADS, A_HD)

    f = lb + (1.0 - lb) * jax.nn.sigmoid(fb.astype(jnp.float32))
    logf = jnp.log(f).reshape(B, L, B_HEADS, B_DK)
    kb = (1.0 - f).reshape(B, L, B_HEADS, B_DK)
    qb = jax.nn.silu(qb.astype(jnp.float32)).reshape(B, L, B_HEADS, B_DK)
    ib = ib.astype(jnp.float32).reshape(B, L, B_HEADS, B_DV)
    return heads_a(qa), heads_a(ka), heads_a(va), qb, logf, kb, ib, gb


def _mix_out(oa, ob, gb, beta_a, gnorm_b, w_out):
    B, L = oa.shape[:2]
    oa = rms_norm(oa.reshape(B, L, A_WIDTH), beta_a)
    ob = rms_norm(ob, gnorm_b) * jax.nn.silu(gb.reshape(B, L, B_HEADS, B_DV))
    return jnp.concatenate([oa, ob.reshape(B, L, B_WIDTH)], axis=-1) @ w_out


def mixing(h, lb, w_in, beta_a, gnorm_b, w_out, s0, win_k=None, win_v=None):
    qa, ka, va, qb, logf, kb, ib, gb = _mix_proj(h, w_in, lb)
    if win_k is None:
        oa = dilated_attention_prompt(qa, ka, va)
        keep = min(A_MAX_WINDOW, h.shape[1])
        rows_k, rows_v = ka[:, -keep:], va[:, -keep:]
    else:
        oa = dilated_attention_sample(qa, ka, va, win_k, win_v)
        rows_k, rows_v = ka, va
    ob, s_new = hgrn2_chunked(qb, logf, kb, ib, s0)
    y = _mix_out(oa.astype(h.dtype), ob.astype(h.dtype), gb, beta_a, gnorm_b, w_out)
    return y, rows_k, rows_v, s_new.astype(h.dtype)


def memory_kv(mem, g_mem, w_mk, w_mv):
    B = mem.shape[0]
    m = rms_norm(mem, g_mem)
    return (m @ w_mk).reshape(B, N_MEM, X_HEADS, X_HD), (m @ w_mv).reshape(B, N_MEM, X_HEADS, X_HD)


def cross_attention(h, mk, mv, w_cq, w_co):
    B, L, _ = h.shape
    q = (h @ w_cq).reshape(B, L, X_HEADS, X_HD)
    s = jnp.einsum('blhd,bmhd->bhlm', q, mk, preferred_element_type=jnp.float32) * (X_HD ** -0.5)
    p = jax.nn.softmax(s, axis=-1).astype(mv.dtype)
    o = jnp.einsum('bhlm,bmhd->blhd', p, mv).reshape(B, L, X_WIDTH)
    return o @ w_co


def peer_ffn(h, w_pq, peer_k1, peer_k2, peer_u, peer_v):
    shp = h.shape
    x = h.reshape(-1, D_MODEL)
    T = x.shape[0]
    blk = min(PEER_BLOCK, T)
    nb = -(-T // blk)
    xb = jnp.pad(x, ((0, nb * blk - T), (0, 0))).reshape(nb, blk, D_MODEL)

    def one_block(xc):
        q = (xc @ w_pq).reshape(blk, PEER_HEADS, PEER_QDIM)
        s1 = jnp.einsum('thd,hnd->thn', q[..., :PEER_HALF], peer_k1, preferred_element_type=jnp.float32)
        s2 = jnp.einsum('thd,hnd->thn', q[..., PEER_HALF:], peer_k2, preferred_element_type=jnp.float32)
        v1, i1 = lax.top_k(s1, PEER_TOPK)
        v2, i2 = lax.top_k(s2, PEER_TOPK)
        cand = (v1[..., :, None] + v2[..., None, :]).reshape(blk, PEER_HEADS, PEER_TOPK * PEER_TOPK)
        sc, ci = lax.top_k(cand, PEER_TOPK)
        e = (jnp.take_along_axis(i1, ci // PEER_TOPK, axis=-1) * PEER_NKEYS
             + jnp.take_along_axis(i2, ci % PEER_TOPK, axis=-1))
        g = jax.nn.softmax(sc, axis=-1)
        u = peer_u[e]
        a = jax.nn.gelu(jnp.einsum('td,thkd->thk', xc, u, preferred_element_type=jnp.float32), approximate=False)
        return jnp.einsum('thk,thkd->td', (g * a).astype(xc.dtype), peer_v[e])

    y = lax.map(one_block, xb).reshape(nb * blk, D_MODEL)[:T]
    return y.reshape(shp)


def setup_inputs(seed: int = 0) -> dict:
    key = jax.random.key(seed)
    ks = iter(jax.random.split(key, 40))

    def nrm(shape, scale):
        return jax.random.normal(next(ks), shape, jnp.float32) * scale

    def gain(shape):
        return 1.0 + 0.05 * jax.random.normal(next(ks), shape, jnp.float32)

    w_buf = min(A_MAX_WINDOW, PAST_LEN)
    return {
        'x_prompt': nrm((BATCH, SEQ, D_MODEL), 1.0),
        'x_sample': nrm((DEC_BATCH, DEC_SEQ, D_MODEL), 1.0),
        'cache_swa_k': nrm((DEPTH, DEC_BATCH, w_buf, A_HEADS, A_HD), 1.0),
        'cache_swa_v': nrm((DEPTH, DEC_BATCH, w_buf, A_HEADS, A_HD), 1.0),
        'state_hgrn': nrm((DEPTH, DEC_BATCH, B_HEADS, B_DK, B_DV), 0.5),
        'cache_mem_k': nrm((DEPTH, DEC_BATCH, N_MEM, X_HEADS, X_HD), 1.0),
        'cache_mem_v': nrm((DEPTH, DEC_BATCH, N_MEM, X_HEADS, X_HD), 1.0),
        'mem_prompt': nrm((BATCH, N_MEM, D_MODEL), 1.0),
        'norm_mix': gain((DEPTH, D_MODEL)),
        'w_in': nrm((DEPTH, D_MODEL, MIX_IN), D_MODEL ** -0.5),
        'lb_logits': nrm((DEPTH + 1, B_HEADS * B_DK), 0.5),
        'beta_a': gain((DEPTH, A_WIDTH)),
        'gnorm_b': gain((DEPTH, B_HEADS, B_DV)),
        'w_out': nrm((DEPTH, MIX_WIDTH, D_MODEL), MIX_WIDTH ** -0.5),
        'norm_cross': gain((DEPTH, D_MODEL)),
        'norm_mem': gain((DEPTH, D_MODEL)),
        'w_cq': nrm((DEPTH, D_MODEL, X_WIDTH), D_MODEL ** -0.5),
        'w_mk': nrm((DEPTH, D_MODEL, X_WIDTH), D_MODEL ** -0.5),
        'w_mv': nrm((DEPTH, D_MODEL, X_WIDTH), D_MODEL ** -0.5),
        'w_co': nrm((DEPTH, X_WIDTH, D_MODEL), X_WIDTH ** -0.5),
        'norm_ffn': gain((DEPTH, D_MODEL)),
        'w_pq': nrm((DEPTH, D_MODEL, PEER_HEADS * PEER_QDIM), D_MODEL ** -0.5),
        'peer_k1': nrm((DEPTH, PEER_HEADS, PEER_NKEYS, PEER_HALF), PEER_HALF ** -0.5),
        'peer_k2': nrm((DEPTH, PEER_HEADS, PEER_NKEYS, PEER_HALF), PEER_HALF ** -0.5),
        'peer_u': nrm((DEPTH, PEER_N, D_MODEL), D_MODEL ** -0.5),
        'peer_v': nrm((DEPTH, PEER_N, D_MODEL), PEER_HEADS ** -0.5),
        'norm_final': gain((D_MODEL,)),
    }


def reference(x_prompt, x_sample, cache_swa_k, cache_swa_v, state_hgrn, cache_mem_k, cache_mem_v,
              mem_prompt, norm_mix, w_in, lb_logits, beta_a, gnorm_b, w_out, norm_cross, norm_mem,
              w_cq, w_mk, w_mv, w_co, norm_ffn, w_pq, peer_k1, peer_k2, peer_u, peer_v, norm_final):
    lb_all = jnp.cumsum(jax.nn.softmax(lb_logits.astype(jnp.float32), axis=0), axis=0)
    xp, xs = x_prompt, x_sample
    p_k, p_v, p_s, p_mk, p_mv, s_k, s_v, s_s = [], [], [], [], [], [], [], []
    for l in range(DEPTH):
        s0 = jnp.zeros((xp.shape[0], B_HEADS, B_DK, B_DV), jnp.float32)
        y, rk, rv, st = mixing(rms_norm(xp, norm_mix[l]), lb_all[l], w_in[l], beta_a[l], gnorm_b[l], w_out[l], s0)
        xp = xp + y
        p_k.append(rk)
        p_v.append(rv)
        p_s.append(st)
        y, rk, rv, st = mixing(rms_norm(xs, norm_mix[l]), lb_all[l], w_in[l], beta_a[l], gnorm_b[l], w_out[l],
                               state_hgrn[l], cache_swa_k[l], cache_swa_v[l])
        xs = xs + y
        s_k.append(rk)
        s_v.append(rv)
        s_s.append(st)
        mk, mv = memory_kv(mem_prompt, norm_mem[l], w_mk[l], w_mv[l])
        p_mk.append(mk)
        p_mv.append(mv)
        xp = xp + cross_attention(rms_norm(xp, norm_cross[l]), mk, mv, w_cq[l], w_co[l])
        xs = xs + cross_attention(rms_norm(xs, norm_cross[l]), cache_mem_k[l], cache_mem_v[l], w_cq[l], w_co[l])
        xp = xp + peer_ffn(rms_norm(xp, norm_ffn[l]), w_pq[l], peer_k1[l], peer_k2[l], peer_u[l], peer_v[l])
        xs = xs + peer_ffn(rms_norm(xs, norm_ffn[l]), w_pq[l], peer_k1[l], peer_k2[l], peer_u[l], peer_v[l])
    y_prompt = rms_norm(xp, norm_final)
    y_sample = rms_norm(xs, norm_final)
    return (y_prompt, y_sample, jnp.stack(p_k), jnp.stack(p_v), jnp.stack(p_s), jnp.stack(p_mk),
            jnp.stack(p_mv), jnp.stack(s_k), jnp.stack(s_v), jnp.stack(s_s))
```

```python
import functools

import jax
import jax.numpy as jnp
import numpy as np
from jax import lax
from jax.experimental import pallas as pl
from jax.experimental.pallas import tpu as pltpu

F32 = jnp.float32
BF16 = jnp.bfloat16
EPS = 1e-6
NEG_INF = float("-inf")

LANES = 128
D_MODEL = 1024
A_HEADS, A_HD = 8, 64
A_WIDTH = A_HEADS * A_HD
BRANCHES = ((128, 1), (512, 4), (2048, 16))
SPAN = 128
B_HEADS, B_DK, B_DV = 4, 128, 128
B_WIDTH = B_HEADS * B_DV
MIX_IN = 3 * A_WIDTH + 2 * B_HEADS * B_DK + 2 * B_WIDTH
N_MEM = 256
X_HEADS, X_HD = 4, 128
X_WIDTH = X_HEADS * X_HD
PEER_HEADS = 8
PEER_NKEYS = 128
PEER_N = PEER_NKEYS * PEER_NKEYS
PEER_HALF = 128
PEER_TOPK = 16
VMEM_LIMIT = 56 * 1024 * 1024


def _params(*sem):
    return pltpu.CompilerParams(dimension_semantics=sem, vmem_limit_bytes=VMEM_LIMIT)


def _rms(x, g):
    return x * lax.rsqrt(jnp.mean(x * x, axis=-1, keepdims=True) + EPS) * g


def _dot(a, b):
    return jnp.dot(a, b, preferred_element_type=F32)


def _dot_nt(a, b):
    return lax.dot_general(a, b, (((1,), (1,)), ((), ())), preferred_element_type=F32)


def _norm_matmul_kernel(x_ref, g_ref, w_ref, o_ref):
    h = _rms(x_ref[...], g_ref[...])
    o_ref[...] = _dot(h.astype(BF16), w_ref[...])


def norm_matmul(x, g, w, tb):
    n, d = x.shape
    m = w.shape[1]
    return pl.pallas_call(
        _norm_matmul_kernel,
        grid=(n // tb,),
        in_specs=[pl.BlockSpec((tb, d), lambda i: (i, 0)),
                  pl.BlockSpec((1, d), lambda i: (0, 0)),
                  pl.BlockSpec((d, m), lambda i: (0, 0))],
        out_specs=pl.BlockSpec((tb, m), lambda i: (i, 0)),
        out_shape=jax.ShapeDtypeStruct((n, m), F32),
        compiler_params=_params("parallel"),
        name="norm_matmul",
    )(x, g.reshape(1, d), w)


def _attn_prompt_kernel(q_ref, k_ref, v_ref, o_ref, l_ref):
    seq = q_ref.shape[0]
    lane = lax.broadcasted_iota(jnp.int32, (1, LANES), 1)
    head0 = lane < A_HD
    qi = lax.broadcasted_iota(jnp.int32, (SPAN, 2 * SPAN), 0)
    kk = lax.broadcasted_iota(jnp.int32, (SPAN, 2 * SPAN), 1)
    dist = SPAN + qi - kk
    band = (dist >= 0) & (dist <= SPAN)
    scale = A_HD ** -0.5

    for bi, (window, dil) in enumerate(BRANCHES):
        assert window // dil == SPAN
        step = SPAN * dil

        def rows(start, dil=dil):
            return pl.ds(start, SPAN, stride=dil) if dil > 1 else pl.ds(start, SPAN)

        def body(idx, carry, bi=bi, dil=dil, step=step, rows=rows):
            r = idx % dil
            blk = idx // dil
            cur = r + blk * step
            prev = r + jnp.maximum(blk - 1, 0) * step
            q = q_ref[rows(cur), :] * scale
            kc = jnp.concatenate([k_ref[rows(prev), :], k_ref[rows(cur), :]], axis=0).astype(BF16)
            vc = jnp.concatenate([v_ref[rows(prev), :], v_ref[rows(cur), :]], axis=0).astype(BF16)
            valid = band & ((kk >= SPAN) | (blk > 0))
            outs, lses = [], []
            for hh in range(2):
                hm = head0 if hh == 0 else jnp.logical_not(head0)
                qh = jnp.where(hm, q, 0.0).astype(BF16)
                s = jnp.where(valid, _dot_nt(qh, kc), NEG_INF)
                m = jnp.max(s, axis=-1, keepdims=True)
                p = jnp.exp(s - m)
                den = jnp.sum(p, axis=-1, keepdims=True)
                outs.append(_dot(p.astype(BF16), vc) / den)
                lses.append(m + jnp.log(den))
            o_new = jnp.where(head0, outs[0], outs[1])
            l_new = jnp.where(head0, lses[0], lses[1])
            if bi == 0:
                o_ref[rows(cur), :] = o_new
                l_ref[rows(cur), :] = l_new
            else:
                o_old = o_ref[rows(cur), :]
                l_old = l_ref[rows(cur), :]
                mx = jnp.maximum(l_old, l_new)
                wa = jnp.exp(l_old - mx)
                wb = jnp.exp(l_new - mx)
                tot = wa + wb
                o_ref[rows(cur), :] = (wa * o_old + wb * o_new) / tot
                l_ref[rows(cur), :] = mx + jnp.log(tot)
            return carry

        lax.fori_loop(0, seq // SPAN, body, 0)


def attn_prompt(z, batch, seq):
    npair = A_WIDTH // LANES
    return pl.pallas_call(
        _attn_prompt_kernel,
        grid=(batch, npair),
        in_specs=[pl.BlockSpec((seq, LANES), lambda b, p: (b, p)),
                  pl.BlockSpec((seq, LANES), lambda b, p: (b, npair + p)),
                  pl.BlockSpec((seq, LANES), lambda b, p: (b, 2 * npair + p))],
        out_specs=pl.BlockSpec((seq, LANES), lambda b, p: (b, p)),
        out_shape=jax.ShapeDtypeStruct((batch * seq, A_WIDTH), F32),
        scratch_shapes=[pltpu.VMEM((seq, LANES), F32)],
        compiler_params=_params("parallel", "parallel"),
        name="attn_prompt",
    )(z, z, z)


def _attn_sample_kernel(q_ref, kn_ref, vn_ref, kc_ref, vc_ref, o_ref, *, n_new):
    win = kc_ref.shape[1]
    rows8 = q_ref.shape[1]
    nrow = A_HEADS * rows8
    q8 = q_ref[0] * (A_HD ** -0.5)
    qe = jnp.concatenate([q8] * A_HEADS, axis=0)
    row = lax.broadcasted_iota(jnp.int32, (nrow, A_WIDTH), 0)
    col = lax.broadcasted_iota(jnp.int32, (nrow, A_WIDTH), 1)
    own = (col // A_HD) == (row // rows8)
    qe = jnp.where(own, qe, 0.0).astype(BF16)

    def multiplicity(delta, ok):
        mult = jnp.zeros(delta.shape, F32)
        for window, dil in BRANCHES:
            hit = ok & (delta >= 0) & ((delta & (dil - 1)) == 0) & (delta <= window)
            mult = mult + hit.astype(F32)
        return mult

    t_c = lax.broadcasted_iota(jnp.int32, (nrow, win), 0) % rows8
    p_c = lax.broadcasted_iota(jnp.int32, (nrow, win), 1)
    mult_c = multiplicity(win + t_c - p_c, p_c >= 0)
    t_n = lax.broadcasted_iota(jnp.int32, (nrow, rows8), 0) % rows8
    u_n = lax.broadcasted_iota(jnp.int32, (nrow, rows8), 1)
    mult_n = multiplicity(t_n - u_n, u_n < n_new)

    s_c = jnp.where(mult_c > 0, _dot_nt(qe, kc_ref[0].astype(BF16)), NEG_INF)
    s_n = jnp.where(mult_n > 0, _dot_nt(qe, kn_ref[0].astype(BF16)), NEG_INF)
    m = jnp.maximum(jnp.max(s_c, axis=-1, keepdims=True), jnp.max(s_n, axis=-1, keepdims=True))
    e_c = mult_c * jnp.exp(s_c - m)
    e_n = mult_n * jnp.exp(s_n - m)
    den = jnp.sum(e_c, axis=-1, keepdims=True) + jnp.sum(e_n, axis=-1, keepdims=True)
    o = (_dot(e_c.astype(BF16), vc_ref[0].astype(BF16)) + _dot(e_n.astype(BF16), vn_ref[0].astype(BF16))) / den
    o = jnp.where(own, o, 0.0)
    acc = o[0:rows8]
    for h in range(1, A_HEADS):
        acc = acc + o[h * rows8:(h + 1) * rows8]
    o_ref[0] = acc


def attn_sample(z8, cache_k, cache_v, n_new):
    nb, rows8, _ = z8.shape
    win = cache_k.shape[1]
    return pl.pallas_call(
        functools.partial(_attn_sample_kernel, n_new=n_new),
        grid=(nb,),
        in_specs=[pl.BlockSpec((1, rows8, A_WIDTH), lambda b: (b, 0, 0)),
                  pl.BlockSpec((1, rows8, A_WIDTH), lambda b: (b, 0, 1)),
                  pl.BlockSpec((1, rows8, A_WIDTH), lambda b: (b, 0, 2)),
                  pl.BlockSpec((1, win, A_WIDTH), lambda b: (b, 0, 0)),
                  pl.BlockSpec((1, win, A_WIDTH), lambda b: (b, 0, 0))],
        out_specs=pl.BlockSpec((1, rows8, A_WIDTH), lambda b: (b, 0, 0)),
        out_shape=jax.ShapeDtypeStruct((nb, rows8, A_WIDTH), F32),
        compiler_params=_params("parallel"),
        name="attn_sample",
    )(z8, z8, z8, cache_k, cache_v)


def _hgrn_kernel(q_ref, f_ref, i_ref, lb_ref, s0_ref, o_ref, sn_ref, st_ref, *, chunk, sub, n_valid, has_s0):
    c_idx = pl.program_id(1)

    @pl.when(c_idx == 0)
    def _():
        for h in range(B_HEADS):
            st_ref[h] = s0_ref[0, h].T if has_s0 else jnp.zeros((B_DV, B_DK), F32)

    lb = lb_ref[...]
    f = lb + (1.0 - lb) * jax.nn.sigmoid(f_ref[...])
    logf = jnp.log(f)
    kgate = 1.0 - f
    q = jax.nn.silu(q_ref[...])
    if n_valid < chunk:
        live = lax.broadcasted_iota(jnp.int32, (chunk, 1), 0) < n_valid
        logf = jnp.where(live, logf, 0.0)
        kgate = jnp.where(live, kgate, 0.0)
        q = jnp.where(live, q, 0.0)
    iv = i_ref[...]
    tr = lax.broadcasted_iota(jnp.int32, (chunk, chunk), 0)
    tc = lax.broadcasted_iota(jnp.int32, (chunk, chunk), 1)
    tril = (tr >= tc).astype(F32)
    cum = jnp.dot(tril, logf, preferred_element_type=F32, precision=lax.Precision.HIGHEST)

    nsub = chunk // sub
    srow = lax.broadcasted_iota(jnp.int32, (sub, B_DK), 0)
    krow = lax.broadcasted_iota(jnp.int32, (chunk, 1), 0)
    e_r = lax.broadcasted_iota(jnp.int32, (sub * B_DK, chunk), 0) // B_DK
    e_c = lax.broadcasted_iota(jnp.int32, (sub * B_DK, chunk), 1)

    for h in range(B_HEADS):
        sl = slice(h * B_DK, (h + 1) * B_DK)
        c_h, q_h, k_h, i_h = cum[:, sl], q[:, sl], kgate[:, sl], iv[:, sl]
        st = st_ref[h]
        i_b = i_h.astype(BF16)
        o_inter = _dot_nt((q_h * jnp.exp(c_h)).astype(BF16), st.astype(BF16))
        a_rows = []
        for blk in range(nsub):
            r0 = blk * sub
            c_i, q_i, k_i = c_h[r0:r0 + sub], q_h[r0:r0 + sub], k_h[r0:r0 + sub]
            parts = []
            for s in range(sub):
                dec = jnp.exp(jnp.where(srow >= s, c_i - c_i[s:s + 1], NEG_INF))
                parts.append((q_i * k_i[s:s + 1] * dec).astype(BF16))
            onehot = (e_c == r0 + e_r).astype(BF16)
            a_blk = _dot(jnp.concatenate(parts, axis=1), onehot)
            if blk > 0:
                c_ref = c_h[r0 - 1:r0]
                q_t = (q_i * jnp.exp(c_i - c_ref)).astype(BF16)
                k_t = jnp.where(krow < r0, k_h * jnp.exp(jnp.minimum(c_ref - c_h, 0.0)), 0.0).astype(BF16)
                a_blk = a_blk + _dot_nt(q_t, k_t)
            a_rows.append(a_blk)
        a_full = a_rows[0] if nsub == 1 else jnp.concatenate(a_rows, axis=0)
        o_ref[:, sl] = o_inter + _dot(a_full.astype(BF16), i_b)
        c_last = c_h[chunk - 1:chunk]
        k_dec = (k_h * jnp.exp(c_last - c_h)).astype(BF16)
        st_ref[h] = st * jnp.exp(c_last) + _dot(i_h.T.astype(BF16), k_dec)

    @pl.when(c_idx == pl.num_programs(1) - 1)
    def _():
        for h in range(B_HEADS):
            sn_ref[0, h] = st_ref[h].T


def hgrn(z, lb, s0, batch, seq, chunk, sub, n_valid):
    nchunk = seq // chunk
    qcol = 3 * A_WIDTH // B_WIDTH
    has_s0 = s0 is not None
    if s0 is None:
        s0 = jnp.zeros((1, B_HEADS, B_DK, B_DV), F32)
    s0_map = (lambda b, c: (b, 0, 0, 0)) if has_s0 else (lambda b, c: (0, 0, 0, 0))
    tok = lambda col: pl.BlockSpec((chunk, B_WIDTH), lambda b, c, col=col: (b * nchunk + c, col))
    return pl.pallas_call(
        functools.partial(_hgrn_kernel, chunk=chunk, sub=sub, n_valid=n_valid, has_s0=has_s0),
        grid=(batch, nchunk),
        in_specs=[tok(qcol), tok(qcol + 1), tok(qcol + 2),
                  pl.BlockSpec((1, B_WIDTH), lambda b, c: (0, 0)),
                  pl.BlockSpec((1, B_HEADS, B_DK, B_DV), s0_map)],
        out_specs=[pl.BlockSpec((chunk, B_WIDTH), lambda b, c: (b * nchunk + c, 0)),
                   pl.BlockSpec((1, B_HEADS, B_DK, B_DV), lambda b, c: (b, 0, 0, 0))],
        out_shape=[jax.ShapeDtypeStruct((batch * seq, B_WIDTH), F32),
                   jax.ShapeDtypeStruct((batch, B_HEADS, B_DK, B_DV), F32)],
        scratch_shapes=[pltpu.VMEM((B_HEADS, B_DV, B_DK), F32)],
        compiler_params=_params("parallel", "arbitrary"),
        name="hgrn",
    )(z, z, z, lb, s0)


def _mix_out_kernel(x_ref, oa_ref, ob_ref, gb_ref, beta_ref, gn_ref, wout_ref, nc_ref, wcq_ref, x1_ref, qc_ref):
    parts = [_rms(oa_ref[...], beta_ref[...]).astype(BF16)]
    ob = ob_ref[...]
    gate = jax.nn.silu(gb_ref[...])
    gn = gn_ref[...]
    for h in range(B_HEADS):
        sl = slice(h * B_DV, (h + 1) * B_DV)
        parts.append((_rms(ob[:, sl], gn[:, sl]) * gate[:, sl]).astype(BF16))
    x1 = x_ref[...] + _dot(jnp.concatenate(parts, axis=-1), wout_ref[...])
    x1_ref[...] = x1
    qc_ref[...] = _dot(_rms(x1, nc_ref[...]).astype(BF16), wcq_ref[...])


def mix_out(x, oa, ob, z, beta, gn, wout, nc, wcq, tb):
    n = x.shape[0]
    gcol = MIX_IN // B_WIDTH - 1
    row = lambda w: pl.BlockSpec((tb, w), lambda i: (i, 0))
    const = lambda a: pl.BlockSpec(a.shape, lambda i: (0,) * a.ndim)
    args = (beta.reshape(1, A_WIDTH), gn.reshape(1, B_WIDTH), wout, nc.reshape(1, D_MODEL), wcq)
    return pl.pallas_call(
        _mix_out_kernel,
        grid=(n // tb,),
        in_specs=[row(D_MODEL), row(A_WIDTH), row(B_WIDTH), pl.BlockSpec((tb, B_WIDTH), lambda i: (i, gcol))]
        + [const(a) for a in args],
        out_specs=[row(D_MODEL), row(X_WIDTH)],
        out_shape=[jax.ShapeDtypeStruct((n, D_MODEL), F32), jax.ShapeDtypeStruct((n, X_WIDTH), F32)],
        compiler_params=_params("parallel"),
        name="mix_out",
    )(x, oa, ob, z, *args)


def _cross_kernel(q_ref, k_ref, v_ref, o_ref):
    scale = X_HD ** -0.5
    for h in range(X_HEADS):
        sl = slice(h * X_HD, (h + 1) * X_HD)
        s = _dot_nt(q_ref[:, sl].astype(BF16), k_ref[:, sl].astype(BF16)) * scale
        m = jnp.max(s, axis=-1, keepdims=True)
        p = jnp.exp(s - m)
        den = jnp.sum(p, axis=-1, keepdims=True)
        o_ref[:, sl] = _dot(p.astype(BF16), v_ref[:, sl].astype(BF16)) / den


def cross(qc, mk, mv, kcol, vcol, batch, tq):
    n = qc.shape[0]
    per = n // batch // tq
    return pl.pallas_call(
        _cross_kernel,
        grid=(batch, per),
        in_specs=[pl.BlockSpec((tq, X_WIDTH), lambda b, i: (b * per + i, 0)),
                  pl.BlockSpec((N_MEM, X_WIDTH), lambda b, i: (b, kcol)),
                  pl.BlockSpec((N_MEM, X_WIDTH), lambda b, i: (b, vcol))],
        out_specs=pl.BlockSpec((tq, X_WIDTH), lambda b, i: (b * per + i, 0)),
        out_shape=jax.ShapeDtypeStruct((n, X_WIDTH), F32),
        compiler_params=_params("parallel", "parallel"),
        name="cross",
    )(qc, mk, mv)


def _top_values(s, k):
    vals = []
    for _ in range(k):
        m = jnp.max(s, axis=0, keepdims=True)
        vals.append(m)
        s = jnp.where(s == m, NEG_INF, s)
    return jnp.concatenate(vals, axis=0)


def _peer_pre_kernel(x1_ref, oc_ref, wco_ref, nf_ref, wpq_ref, k1_ref, k2_ref,
                     x2_ref, ht_ref, s1_ref, s2_ref, tau_ref, csh_ref):
    x2 = x1_ref[...] + _dot(oc_ref[...].astype(BF16), wco_ref[...])
    x2_ref[...] = x2
    ht = _rms(x2, nf_ref[...]).T.astype(BF16)
    ht_ref[...] = ht
    qt = _dot(wpq_ref[...], ht)
    hi = lax.Precision.HIGHEST
    for h in range(PEER_HEADS):
        r0 = h * 2 * PEER_HALF
        s1 = jnp.dot(k1_ref[h], qt[r0:r0 + PEER_HALF], preferred_element_type=F32, precision=hi)
        s2 = jnp.dot(k2_ref[h], qt[r0 + PEER_HALF:r0 + 2 * PEER_HALF], preferred_element_type=F32, precision=hi)
        s1_ref[h] = s1
        s2_ref[h] = s2
        v1 = _top_values(s1, PEER_TOPK)
        v2 = _top_values(s2, PEER_TOPK)
        cand = jnp.concatenate([v1[a:a + 1] + v2 for a in range(PEER_TOPK)], axis=0)
        sc = _top_values(cand, PEER_TOPK)
        top = sc[0:1]
        den = jnp.sum(jnp.exp(sc - top), axis=0, keepdims=True)
        tau_ref[h:h + 1, :] = sc[PEER_TOPK - 1:PEER_TOPK]
        csh_ref[h:h + 1, :] = top + jnp.log(den)


def peer_pre(x1, oc, wco, nf, wpq_t, k1, k2, tb):
    n = x1.shape[0]
    const = lambda a: pl.BlockSpec(a.shape, lambda i: (0,) * a.ndim)
    args = (wco, nf.reshape(1, D_MODEL), wpq_t, k1, k2)
    return pl.pallas_call(
        _peer_pre_kernel,
        grid=(n // tb,),
        in_specs=[pl.BlockSpec((tb, D_MODEL), lambda i: (i, 0)), pl.BlockSpec((tb, X_WIDTH), lambda i: (i, 0))]
        + [const(a) for a in args],
        out_specs=[pl.BlockSpec((tb, D_MODEL), lambda i: (i, 0)),
                   pl.BlockSpec((D_MODEL, tb), lambda i: (0, i)),
                   pl.BlockSpec((PEER_HEADS, PEER_NKEYS, tb), lambda i: (0, 0, i)),
                   pl.BlockSpec((PEER_HEADS, PEER_NKEYS, tb), lambda i: (0, 0, i)),
                   pl.BlockSpec((PEER_HEADS, tb), lambda i: (0, i)),
                   pl.BlockSpec((PEER_HEADS, tb), lambda i: (0, i))],
        out_shape=[jax.ShapeDtypeStruct((n, D_MODEL), F32),
                   jax.ShapeDtypeStruct((D_MODEL, n), BF16),
                   jax.ShapeDtypeStruct((PEER_HEADS, PEER_NKEYS, n), F32),
                   jax.ShapeDtypeStruct((PEER_HEADS, PEER_NKEYS, n), F32),
                   jax.ShapeDtypeStruct((PEER_HEADS, n), F32),
                   jax.ShapeDtypeStruct((PEER_HEADS, n), F32)],
        compiler_params=_params("parallel"),
        name="peer_pre",
    )(x1, oc, *args)


def _peer_dense_kernel(ht_ref, s1_ref, s2_ref, tau_ref, csh_ref, u_ref, vt_ref, x2_ref, g_ref,
                       y_ref, acc_ref, ct_ref, *, eb, tb):
    j = pl.program_id(1)

    @pl.when(j == 0)
    def _():
        acc_ref[...] = jnp.zeros_like(acc_ref)

    at = _dot(u_ref[...], ht_ref[...])
    for ii in range(eb // PEER_NKEYS):
        for tc in range(tb // LANES):
            ls = slice(tc * LANES, (tc + 1) * LANES)
            w = jnp.zeros((PEER_NKEYS, LANES), F32)
            for h in range(PEER_HEADS):
                sm = s2_ref[h, :, ls] + s1_ref[h, ii:ii + 1, ls]
                gate = jnp.exp(sm - csh_ref[h:h + 1, ls])
                w = w + jnp.where(sm >= tau_ref[h:h + 1, ls], gate, 0.0)
            a = at[ii * PEER_NKEYS:(ii + 1) * PEER_NKEYS, ls]
            gelu = 0.5 * a * (1.0 + lax.erf(a * (2.0 ** -0.5)))
            ct_ref[ii * PEER_NKEYS:(ii + 1) * PEER_NKEYS, ls] = (w * gelu).astype(BF16)
    acc_ref[...] += _dot(vt_ref[...], ct_ref[...])

    @pl.when(j == pl.num_programs(1) - 1)
    def _():
        y_ref[...] = _rms(x2_ref[...] + acc_ref[...].T, g_ref[...])


def peer_dense(ht, s1, s2, tau, csh, u, vt, x2, g, tb, eb):
    n = x2.shape[0]
    tok = lambda rows: pl.BlockSpec((rows, tb), lambda i, j: (0, i))
    tok3 = pl.BlockSpec((PEER_HEADS, PEER_NKEYS, tb), lambda i, j: (0, 0, i))
    return pl.pallas_call(
        functools.partial(_peer_dense_kernel, eb=eb, tb=tb),
        grid=(n // tb, PEER_N // eb),
        in_specs=[tok(D_MODEL), pl.BlockSpec((PEER_HEADS, eb // PEER_NKEYS, tb), lambda i, j: (0, j, i)), tok3,
                  tok(PEER_HEADS), tok(PEER_HEADS),
                  pl.BlockSpec((eb, D_MODEL), lambda i, j: (j, 0)),
                  pl.BlockSpec((D_MODEL, eb), lambda i, j: (0, j)),
                  pl.BlockSpec((tb, D_MODEL), lambda i, j: (i, 0)),
                  pl.BlockSpec((1, D_MODEL), lambda i, j: (0, 0))],
        out_specs=pl.BlockSpec((tb, D_MODEL), lambda i, j: (i, 0)),
        out_shape=jax.ShapeDtypeStruct((n, D_MODEL), F32),
        scratch_shapes=[pltpu.VMEM((D_MODEL, tb), F32), pltpu.VMEM((eb, tb), BF16)],
        compiler_params=_params("parallel", "arbitrary"),
        name="peer_dense",
    )(ht, s1, s2, tau, csh, u, vt, x2, g.reshape(1, D_MODEL))


def _trunk(x, z, oa, ob, mk, mv, kcol, vcol, batch, tq, w, tb_tok, tb_peer, pad_rows):
    x1, qc = mix_out(x, oa, ob, z, w["beta_a"], w["gnorm_b"], w["w_out"], w["norm_cross"], w["w_cq"], tb_tok)
    if pad_rows:
        n_new = x.shape[0] // batch
        qc = jnp.pad(qc.reshape(batch, n_new, X_WIDTH), ((0, 0), (0, pad_rows - n_new), (0, 0)))
        oc = cross(qc.reshape(batch * pad_rows, X_WIDTH), mk, mv, kcol, vcol, batch, tq)
        oc = oc.reshape(batch, pad_rows, X_WIDTH)[:, :n_new].reshape(x.shape[0], X_WIDTH)
    else:
        oc = cross(qc, mk, mv, kcol, vcol, batch, tq)
    x2, ht, s1, s2, tau, csh = peer_pre(x1, oc, w["w_co"], w["norm_ffn"], w["w_pq_t"], w["k1"], w["k2"], tb_tok)
    return peer_dense(ht, s1, s2, tau, csh, w["u"], w["v_t"], x2, w["norm_final"], tb_peer, 1024)


def kernel(x_prompt, x_sample, cache_swa_k, cache_swa_v, state_hgrn, cache_mem_k, cache_mem_v, mem_prompt, norm_mix, w_in, lb_logits, beta_a, gnorm_b, w_out, norm_cross, norm_mem, w_cq, w_mk, w_mv, w_co, norm_ffn, w_pq, peer_k1, peer_k2, peer_u, peer_v, norm_final):
    batch, seq, _ = x_prompt.shape
    dbatch, dseq, _ = x_sample.shape
    win = cache_swa_k.shape[2]
    rows8 = 8
    layer = 0
    lb = jnp.cumsum(jax.nn.softmax(lb_logits.astype(F32), axis=0), axis=0)[layer].reshape(1, B_WIDTH)
    w = {
        "beta_a": beta_a[layer], "gnorm_b": gnorm_b[layer], "w_out": w_out[layer].astype(BF16),
        "norm_cross": norm_cross[layer], "w_cq": w_cq[layer].astype(BF16), "w_co": w_co[layer].astype(BF16),
        "norm_ffn": norm_ffn[layer], "w_pq_t": w_pq[layer].T.astype(BF16),
        "k1": peer_k1[layer], "k2": peer_k2[layer],
        "u": peer_u[layer].astype(BF16), "v_t": peer_v[layer].T.astype(BF16), "norm_final": norm_final,
    }
    w_in_b = w_in[layer].astype(BF16)
    w_mem = jnp.concatenate([w_mk[layer], w_mv[layer]], axis=1).astype(BF16)

    xp = x_prompt.reshape(batch * seq, D_MODEL)
    zp = norm_matmul(xp, norm_mix[layer], w_in_b, 256)
    oa_p = attn_prompt(zp, batch, seq)
    ob_p, st_p = hgrn(zp, lb, None, batch, seq, 64, 16, 64)
    memkv = norm_matmul(mem_prompt.reshape(batch * N_MEM, D_MODEL), norm_mem[layer], w_mem, 256)
    y_p = _trunk(xp, zp, oa_p, ob_p, memkv, memkv, 0, 1, batch, 512, w, 256, 512, 0)

    xs = x_sample.reshape(dbatch * dseq, D_MODEL)
    zs = norm_matmul(xs, norm_mix[layer], w_in_b, dbatch * dseq)
    zs8 = jnp.pad(zs.reshape(dbatch, dseq, MIX_IN), ((0, 0), (0, rows8 - dseq), (0, 0)))
    oa_s = attn_sample(zs8, cache_swa_k[layer].reshape(dbatch, win, A_WIDTH),
                       cache_swa_v[layer].reshape(dbatch, win, A_WIDTH), dseq)
    oa_s = oa_s[:, :dseq].reshape(dbatch * dseq, A_WIDTH)
    ob_s, st_s = hgrn(zs8.reshape(dbatch * rows8, MIX_IN), lb, state_hgrn[layer], dbatch, rows8, rows8, rows8, dseq)
    ob_s = ob_s.reshape(dbatch, rows8, B_WIDTH)[:, :dseq].reshape(dbatch * dseq, B_WIDTH)
    y_s = _trunk(xs, zs, oa_s, ob_s, cache_mem_k[layer].reshape(dbatch * N_MEM, X_WIDTH),
                 cache_mem_v[layer].reshape(dbatch * N_MEM, X_WIDTH), 0, 0, dbatch, rows8, w,
                 dbatch * dseq, dbatch * dseq, rows8)

    keep = min(BRANCHES[-1][0], seq)
    kp = zp[:, A_WIDTH:2 * A_WIDTH].reshape(batch, seq, A_HEADS, A_HD)[:, seq - keep:]
    vp = zp[:, 2 * A_WIDTH:3 * A_WIDTH].reshape(batch, seq, A_HEADS, A_HD)[:, seq - keep:]
    return (y_p.reshape(batch, seq, D_MODEL), y_s.reshape(dbatch, dseq, D_MODEL),
            kp[None], vp[None], st_p[None],
            memkv[:, :X_WIDTH].reshape(1, batch, N_MEM, X_HEADS, X_HD),
            memkv[:, X_WIDTH:].reshape(1, batch, N_MEM, X_HEADS, X_HD),
            zs[:, A_WIDTH:2 * A_WIDTH].reshape(1, dbatch, dseq, A_HEADS, A_HD),
            zs[:, 2 * A_WIDTH:3 * A_WIDTH].reshape(1, dbatch, dseq, A_HEADS, A_HD),
            st_s[None])
```

```python
import functools

import jax
import jax.numpy as jnp
import numpy as np
from jax import lax
from jax.experimental import pallas as pl
from jax.experimental.pallas import tpu as pltpu

F32 = jnp.float32
BF16 = jnp.bfloat16
EPS = 1e-6
NEG_INF = float("-inf")

LANES = 128
D_MODEL = 1024
A_HEADS, A_HD = 8, 64
A_WIDTH = A_HEADS * A_HD
BRANCHES = ((128, 1), (512, 4), (2048, 16))
SPAN = 128
ATTN_UNROLL = 4
B_HEADS, B_DK, B_DV = 4, 128, 128
B_WIDTH = B_HEADS * B_DV
MIX_IN = 3 * A_WIDTH + 2 * B_HEADS * B_DK + 2 * B_WIDTH
N_MEM = 256
X_HEADS, X_HD = 4, 128
X_WIDTH = X_HEADS * X_HD
PEER_HEADS = 8
PEER_NKEYS = 128
PEER_N = PEER_NKEYS * PEER_NKEYS
PEER_HALF = 128
PEER_TOPK = 16
VMEM_LIMIT = 56 * 1024 * 1024


def _params(*sem):
    return pltpu.CompilerParams(dimension_semantics=sem, vmem_limit_bytes=VMEM_LIMIT)


def _rms(x, g):
    return x * lax.rsqrt(jnp.mean(x * x, axis=-1, keepdims=True) + EPS) * g


def _dot(a, b):
    return jnp.dot(a, b, preferred_element_type=F32)


def _dot_nt(a, b):
    return lax.dot_general(a, b, (((1,), (1,)), ((), ())), preferred_element_type=F32)


def _norm_matmul_kernel(x_ref, g_ref, w_ref, o_ref):
    h = _rms(x_ref[...], g_ref[...])
    o_ref[...] = _dot(h.astype(BF16), w_ref[...])


def norm_matmul(x, g, w, tb):
    n, d = x.shape
    m = w.shape[1]
    return pl.pallas_call(
        _norm_matmul_kernel,
        grid=(n // tb,),
        in_specs=[pl.BlockSpec((tb, d), lambda i: (i, 0)),
                  pl.BlockSpec((1, d), lambda i: (0, 0)),
                  pl.BlockSpec((d, m), lambda i: (0, 0))],
        out_specs=pl.BlockSpec((tb, m), lambda i: (i, 0)),
        out_shape=jax.ShapeDtypeStruct((n, m), F32),
        compiler_params=_params("parallel"),
        name="norm_matmul",
    )(x, g.reshape(1, d), w)


def _attn_prompt_kernel(q_ref, k_ref, v_ref, o_ref, l_ref):
    seq = q_ref.shape[0]
    lane = lax.broadcasted_iota(jnp.int32, (1, LANES), 1)
    head0 = lane < A_HD
    qi = lax.broadcasted_iota(jnp.int32, (SPAN, 2 * SPAN), 0)
    kk = lax.broadcasted_iota(jnp.int32, (SPAN, 2 * SPAN), 1)
    dist = SPAN + qi - kk
    band = (dist >= 0) & (dist <= SPAN)
    scale = A_HD ** -0.5

    for bi, (window, dil) in enumerate(BRANCHES):
        assert window // dil == SPAN
        step = SPAN * dil

        def rows(start, dil=dil):
            return pl.ds(start, SPAN, stride=dil) if dil > 1 else pl.ds(start, SPAN)

        def block(idx, bi=bi, dil=dil, step=step, rows=rows):
            r = idx % dil
            blk = idx // dil
            cur = r + blk * step
            prev = r + jnp.maximum(blk - 1, 0) * step
            q = q_ref[rows(cur), :] * scale
            kc = jnp.concatenate([k_ref[rows(prev), :], k_ref[rows(cur), :]], axis=0).astype(BF16)
            vc = jnp.concatenate([v_ref[rows(prev), :], v_ref[rows(cur), :]], axis=0).astype(BF16)
            old = (o_ref[rows(cur), :], l_ref[rows(cur), :]) if bi > 0 else None
            valid = band & ((kk >= SPAN) | (blk > 0))
            outs, lses = [], []
            for hh in range(2):
                hm = head0 if hh == 0 else jnp.logical_not(head0)
                qh = jnp.where(hm, q, 0.0).astype(BF16)
                s = jnp.where(valid, _dot_nt(qh, kc), NEG_INF)
                m = jnp.max(s, axis=-1, keepdims=True)
                p = jnp.exp(s - m)
                den = jnp.sum(p, axis=-1, keepdims=True)
                outs.append(_dot(p.astype(BF16), vc) * (1.0 / den))
                lses.append(m + jnp.log(den))
            o_new = jnp.where(head0, outs[0], outs[1])
            l_new = jnp.where(head0, lses[0], lses[1])
            if old is not None:
                o_old, l_old = old
                mx = jnp.maximum(l_old, l_new)
                wa = jnp.exp(l_old - mx)
                wb = jnp.exp(l_new - mx)
                tot = wa + wb
                o_new = (wa * o_old + wb * o_new) * (1.0 / tot)
                l_new = mx + jnp.log(tot)
            return cur, o_new, l_new

        def body(g, carry, rows=rows, block=block):
            done = [block(g * ATTN_UNROLL + u) for u in range(ATTN_UNROLL)]
            for cur, o_new, l_new in done:
                o_ref[rows(cur), :] = o_new
                l_ref[rows(cur), :] = l_new
            return carry

        lax.fori_loop(0, seq // SPAN // ATTN_UNROLL, body, 0)


def attn_prompt(z, batch, seq):
    npair = A_WIDTH // LANES
    return pl.pallas_call(
        _attn_prompt_kernel,
        grid=(batch, npair),
        in_specs=[pl.BlockSpec((seq, LANES), lambda b, p: (b, p)),
                  pl.BlockSpec((seq, LANES), lambda b, p: (b, npair + p)),
                  pl.BlockSpec((seq, LANES), lambda b, p: (b, 2 * npair + p))],
        out_specs=pl.BlockSpec((seq, LANES), lambda b, p: (b, p)),
        out_shape=jax.ShapeDtypeStruct((batch * seq, A_WIDTH), F32),
        scratch_shapes=[pltpu.VMEM((seq, LANES), F32)],
        compiler_params=_params("parallel", "parallel"),
        name="attn_prompt",
    )(z, z, z)


def _attn_sample_kernel(q_ref, kn_ref, vn_ref, kc_ref, vc_ref, o_ref, *, n_new):
    win = kc_ref.shape[1]
    rows8 = q_ref.shape[1]
    nrow = A_HEADS * rows8
    q8 = q_ref[0] * (A_HD ** -0.5)
    qe = jnp.concatenate([q8] * A_HEADS, axis=0)
    row = lax.broadcasted_iota(jnp.int32, (nrow, A_WIDTH), 0)
    col = lax.broadcasted_iota(jnp.int32, (nrow, A_WIDTH), 1)
    own = (col // A_HD) == (row // rows8)
    qe = jnp.where(own, qe, 0.0).astype(BF16)

    def multiplicity(delta, ok):
        mult = jnp.zeros(delta.shape, F32)
        for window, dil in BRANCHES:
            hit = ok & (delta >= 0) & ((delta & (dil - 1)) == 0) & (delta <= window)
            mult = mult + hit.astype(F32)
        return mult

    t_c = lax.broadcasted_iota(jnp.int32, (nrow, win), 0) % rows8
    p_c = lax.broadcasted_iota(jnp.int32, (nrow, win), 1)
    mult_c = multiplicity(win + t_c - p_c, p_c >= 0)
    t_n = lax.broadcasted_iota(jnp.int32, (nrow, rows8), 0) % rows8
    u_n = lax.broadcasted_iota(jnp.int32, (nrow, rows8), 1)
    mult_n = multiplicity(t_n - u_n, u_n < n_new)

    s_c = jnp.where(mult_c > 0, _dot_nt(qe, kc_ref[0].astype(BF16)), NEG_INF)
    s_n = jnp.where(mult_n > 0, _dot_nt(qe, kn_ref[0].astype(BF16)), NEG_INF)
    m = jnp.maximum(jnp.max(s_c, axis=-1, keepdims=True), jnp.max(s_n, axis=-1, keepdims=True))
    e_c = mult_c * jnp.exp(s_c - m)
    e_n = mult_n * jnp.exp(s_n - m)
    den = jnp.sum(e_c, axis=-1, keepdims=True) + jnp.sum(e_n, axis=-1, keepdims=True)
    o = (_dot(e_c.astype(BF16), vc_ref[0].astype(BF16)) + _dot(e_n.astype(BF16), vn_ref[0].astype(BF16))) / den
    o = jnp.where(own, o, 0.0)
    acc = o[0:rows8]
    for h in range(1, A_HEADS):
        acc = acc + o[h * rows8:(h + 1) * rows8]
    o_ref[0] = acc


def attn_sample(z8, cache_k, cache_v, n_new):
    nb, rows8, _ = z8.shape
    win = cache_k.shape[1]
    return pl.pallas_call(
        functools.partial(_attn_sample_kernel, n_new=n_new),
        grid=(nb,),
        in_specs=[pl.BlockSpec((1, rows8, A_WIDTH), lambda b: (b, 0, 0)),
                  pl.BlockSpec((1, rows8, A_WIDTH), lambda b: (b, 0, 1)),
                  pl.BlockSpec((1, rows8, A_WIDTH), lambda b: (b, 0, 2)),
                  pl.BlockSpec((1, win, A_WIDTH), lambda b: (b, 0, 0)),
                  pl.BlockSpec((1, win, A_WIDTH), lambda b: (b, 0, 0))],
        out_specs=pl.BlockSpec((1, rows8, A_WIDTH), lambda b: (b, 0, 0)),
        out_shape=jax.ShapeDtypeStruct((nb, rows8, A_WIDTH), F32),
        compiler_params=_params("parallel"),
        name="attn_sample",
    )(z8, z8, z8, cache_k, cache_v)


def _hgrn_kernel(q_ref, f_ref, i_ref, lb_ref, sel_ref, s0_ref, o_ref, sn_ref, st_ref, *, chunk, sub, n_valid, has_s0):
    c_idx = pl.program_id(1)

    @pl.when(c_idx == 0)
    def _():
        for h in range(B_HEADS):
            st_ref[h] = s0_ref[0, h].T if has_s0 else jnp.zeros((B_DV, B_DK), F32)

    lb = lb_ref[...]
    f = lb + (1.0 - lb) * jax.nn.sigmoid(f_ref[...])
    logf = jnp.log(f)
    kgate = 1.0 - f
    q = jax.nn.silu(q_ref[...])
    if n_valid < chunk:
        live = lax.broadcasted_iota(jnp.int32, (chunk, 1), 0) < n_valid
        logf = jnp.where(live, logf, 0.0)
        kgate = jnp.where(live, kgate, 0.0)
        q = jnp.where(live, q, 0.0)
    iv = i_ref[...]
    tr = lax.broadcasted_iota(jnp.int32, (chunk, chunk), 0)
    tc = lax.broadcasted_iota(jnp.int32, (chunk, chunk), 1)
    tril = (tr >= tc).astype(F32)
    cum = jnp.dot(tril, logf, preferred_element_type=F32, precision=lax.Precision.HIGHEST)

    nsub = chunk // sub
    srow = lax.broadcasted_iota(jnp.int32, (sub, B_DK), 0)
    krow = lax.broadcasted_iota(jnp.int32, (chunk, 1), 0)
    same_sub = (tr // sub) == (tc // sub)
    lane_sum = sel_ref[...]

    for h in range(B_HEADS):
        sl = slice(h * B_DK, (h + 1) * B_DK)
        c_h, q_h, k_h, i_h = cum[:, sl], q[:, sl], kgate[:, sl], iv[:, sl]
        st = st_ref[h]
        i_b = i_h.astype(BF16)
        o_inter = _dot_nt((q_h * jnp.exp(c_h)).astype(BF16), st.astype(BF16))
        diag_rows, off_rows = [], []
        for blk in range(nsub):
            r0 = blk * sub
            c_i, q_i, k_i = c_h[r0:r0 + sub], q_h[r0:r0 + sub], k_h[r0:r0 + sub]
            parts = []
            for s in range(sub):
                dec = jnp.exp(jnp.where(srow >= s, c_i - c_i[s:s + 1], NEG_INF))
                parts.append((q_i * k_i[s:s + 1] * dec).astype(BF16))
            diag_rows.append(jnp.concatenate(parts, axis=1))
            if blk > 0:
                c_ref = c_h[r0 - 1:r0]
                q_t = (q_i * jnp.exp(c_i - c_ref)).astype(BF16)
                k_t = jnp.where(krow < r0, k_h * jnp.exp(jnp.minimum(c_ref - c_h, 0.0)), 0.0).astype(BF16)
                off_rows.append(_dot_nt(q_t, k_t))
            else:
                off_rows.append(jnp.zeros((sub, chunk), F32))
        cat = (lambda xs: xs[0] if nsub == 1 else jnp.concatenate(xs, axis=0))
        a_full = jnp.where(same_sub, _dot(cat(diag_rows), lane_sum), 0.0) + cat(off_rows)
        o_ref[:, sl] = o_inter + _dot(a_full.astype(BF16), i_b)
        c_last = c_h[chunk - 1:chunk]
        k_dec = (k_h * jnp.exp(c_last - c_h)).astype(BF16)
        st_ref[h] = st * jnp.exp(c_last) + _dot(i_h.T.astype(BF16), k_dec)

    @pl.when(c_idx == pl.num_programs(1) - 1)
    def _():
        for h in range(B_HEADS):
            sn_ref[0, h] = st_ref[h].T


def hgrn(z, lb, s0, batch, seq, chunk, sub, n_valid):
    nchunk = seq // chunk
    qcol = 3 * A_WIDTH // B_WIDTH
    has_s0 = s0 is not None
    if s0 is None:
        s0 = jnp.zeros((1, B_HEADS, B_DK, B_DV), F32)
    s0_map = (lambda b, c: (b, 0, 0, 0)) if has_s0 else (lambda b, c: (0, 0, 0, 0))
    tok = lambda col: pl.BlockSpec((chunk, B_WIDTH), lambda b, c, col=col: (b * nchunk + c, col))
    lane_sum = (np.arange(sub * B_DK)[:, None] // B_DK == np.arange(chunk)[None, :] % sub)
    lane_sum = jnp.asarray(lane_sum, dtype=BF16)
    return pl.pallas_call(
        functools.partial(_hgrn_kernel, chunk=chunk, sub=sub, n_valid=n_valid, has_s0=has_s0),
        grid=(batch, nchunk),
        in_specs=[tok(qcol), tok(qcol + 1), tok(qcol + 2),
                  pl.BlockSpec((1, B_WIDTH), lambda b, c: (0, 0)),
                  pl.BlockSpec((sub * B_DK, chunk), lambda b, c: (0, 0)),
                  pl.BlockSpec((1, B_HEADS, B_DK, B_DV), s0_map)],
        out_specs=[pl.BlockSpec((chunk, B_WIDTH), lambda b, c: (b * nchunk + c, 0)),
                   pl.BlockSpec((1, B_HEADS, B_DK, B_DV), lambda b, c: (b, 0, 0, 0))],
        out_shape=[jax.ShapeDtypeStruct((batch * seq, B_WIDTH), F32),
                   jax.ShapeDtypeStruct((batch, B_HEADS, B_DK, B_DV), F32)],
        scratch_shapes=[pltpu.VMEM((B_HEADS, B_DV, B_DK), F32)],
        compiler_params=_params("parallel", "arbitrary"),
        name="hgrn",
    )(z, z, z, lb, lane_sum, s0)


def _mix_out_kernel(x_ref, oa_ref, ob_ref, gb_ref, beta_ref, gn_ref, wout_ref, nc_ref, wcq_ref, x1_ref, qc_ref):
    parts = [_rms(oa_ref[...], beta_ref[...]).astype(BF16)]
    ob = ob_ref[...]
    gate = jax.nn.silu(gb_ref[...])
    gn = gn_ref[...]
    for h in range(B_HEADS):
        sl = slice(h * B_DV, (h + 1) * B_DV)
        parts.append((_rms(ob[:, sl], gn[:, sl]) * gate[:, sl]).astype(BF16))
    x1 = x_ref[...] + _dot(jnp.concatenate(parts, axis=-1), wout_ref[...])
    x1_ref[...] = x1
    qc_ref[...] = _dot(_rms(x1, nc_ref[...]).astype(BF16), wcq_ref[...])


def mix_out(x, oa, ob, z, beta, gn, wout, nc, wcq, tb):
    n = x.shape[0]
    gcol = MIX_IN // B_WIDTH - 1
    row = lambda w: pl.BlockSpec((tb, w), lambda i: (i, 0))
    const = lambda a: pl.BlockSpec(a.shape, lambda i: (0,) * a.ndim)
    args = (beta.reshape(1, A_WIDTH), gn.reshape(1, B_WIDTH), wout, nc.reshape(1, D_MODEL), wcq)
    return pl.pallas_call(
        _mix_out_kernel,
        grid=(n // tb,),
        in_specs=[row(D_MODEL), row(A_WIDTH), row(B_WIDTH), pl.BlockSpec((tb, B_WIDTH), lambda i: (i, gcol))]
        + [const(a) for a in args],
        out_specs=[row(D_MODEL), row(X_WIDTH)],
        out_shape=[jax.ShapeDtypeStruct((n, D_MODEL), F32), jax.ShapeDtypeStruct((n, X_WIDTH), F32)],
        compiler_params=_params("parallel"),
        name="mix_out",
    )(x, oa, ob, z, *args)


def _cross_kernel(q_ref, k_ref, v_ref, o_ref):
    scale = X_HD ** -0.5
    for h in range(X_HEADS):
        sl = slice(h * X_HD, (h + 1) * X_HD)
        s = _dot_nt(q_ref[:, sl].astype(BF16), k_ref[:, sl].astype(BF16)) * scale
        m = jnp.max(s, axis=-1, keepdims=True)
        p = jnp.exp(s - m)
        den = jnp.sum(p, axis=-1, keepdims=True)
        o_ref[:, sl] = _dot(p.astype(BF16), v_ref[:, sl].astype(BF16)) / den


def cross(qc, mk, mv, kcol, vcol, batch, tq):
    n = qc.shape[0]
    per = n // batch // tq
    return pl.pallas_call(
        _cross_kernel,
        grid=(batch, per),
        in_specs=[pl.BlockSpec((tq, X_WIDTH), lambda b, i: (b * per + i, 0)),
                  pl.BlockSpec((N_MEM, X_WIDTH), lambda b, i: (b, kcol)),
                  pl.BlockSpec((N_MEM, X_WIDTH), lambda b, i: (b, vcol))],
        out_specs=pl.BlockSpec((tq, X_WIDTH), lambda b, i: (b * per + i, 0)),
        out_shape=jax.ShapeDtypeStruct((n, X_WIDTH), F32),
        compiler_params=_params("parallel", "parallel"),
        name="cross",
    )(qc, mk, mv)


def _top_values(s, k):
    vals = []
    for _ in range(k):
        m = jnp.max(s, axis=0, keepdims=True)
        vals.append(m)
        s = jnp.where(s == m, NEG_INF, s)
    return jnp.concatenate(vals, axis=0)


def _pair_sums(v1, v2):
    half = PEER_TOPK // 2
    row = lax.broadcasted_iota(jnp.int32, (half, 1), 0)
    parts = [v1[0:1] + v2]
    for a in range(1, half):
        parts.append(jnp.where(row < PEER_TOPK // (a + 1), v1[a:a + 1] + v2[0:half], NEG_INF))
    parts.append(v1[half:] + v2[0:1])
    return jnp.concatenate(parts, axis=0)


def _peer_pre_kernel(x1_ref, oc_ref, wco_ref, nf_ref, wpq_ref, k1_ref, k2_ref,
                     x2_ref, ht_ref, s1_ref, s2_ref, tau_ref):
    x2 = x1_ref[...] + _dot(oc_ref[...].astype(BF16), wco_ref[...])
    x2_ref[...] = x2
    ht = _rms(x2, nf_ref[...]).T.astype(BF16)
    ht_ref[...] = ht
    qt = _dot(wpq_ref[...], ht)
    hi = lax.Precision.HIGHEST
    log2e = float(np.log2(np.e))
    for h in range(PEER_HEADS):
        r0 = h * 2 * PEER_HALF
        s1 = jnp.dot(k1_ref[h], qt[r0:r0 + PEER_HALF], preferred_element_type=F32, precision=hi) * log2e
        s2 = jnp.dot(k2_ref[h], qt[r0 + PEER_HALF:r0 + 2 * PEER_HALF], preferred_element_type=F32, precision=hi) * log2e
        v1 = _top_values(s1, PEER_TOPK)
        v2 = _top_values(s2, PEER_TOPK)
        cand = _pair_sums(v1, v2)
        sc = _top_values(cand, PEER_TOPK)
        top = sc[0:1]
        shift = top + jnp.log2(jnp.sum(jnp.exp2(sc - top), axis=0, keepdims=True))
        picked = cand >= sc[PEER_TOPK - 1:PEER_TOPK]
        tau = jnp.min(jnp.where(picked, _pair_sums(v1, v2 - shift), jnp.inf), axis=0, keepdims=True)
        s1_ref[h] = s1
        s2_ref[h] = s2 - shift
        tau_ref[h:h + 1, :] = tau


def peer_pre(x1, oc, wco, nf, wpq_t, k1, k2, tb):
    n = x1.shape[0]
    const = lambda a: pl.BlockSpec(a.shape, lambda i: (0,) * a.ndim)
    args = (wco, nf.reshape(1, D_MODEL), wpq_t, k1, k2)
    return pl.pallas_call(
        _peer_pre_kernel,
        grid=(n // tb,),
        in_specs=[pl.BlockSpec((tb, D_MODEL), lambda i: (i, 0)), pl.BlockSpec((tb, X_WIDTH), lambda i: (i, 0))]
        + [const(a) for a in args],
        out_specs=[pl.BlockSpec((tb, D_MODEL), lambda i: (i, 0)),
                   pl.BlockSpec((D_MODEL, tb), lambda i: (0, i)),
                   pl.BlockSpec((PEER_HEADS, PEER_NKEYS, tb), lambda i: (0, 0, i)),
                   pl.BlockSpec((PEER_HEADS, PEER_NKEYS, tb), lambda i: (0, 0, i)),
                   pl.BlockSpec((PEER_HEADS, tb), lambda i: (0, i))],
        out_shape=[jax.ShapeDtypeStruct((n, D_MODEL), F32),
                   jax.ShapeDtypeStruct((D_MODEL, n), BF16),
                   jax.ShapeDtypeStruct((PEER_HEADS, PEER_NKEYS, n), F32),
                   jax.ShapeDtypeStruct((PEER_HEADS, PEER_NKEYS, n), F32),
                   jax.ShapeDtypeStruct((PEER_HEADS, n), F32)],
        compiler_params=_params("parallel"),
        name="peer_pre",
    )(x1, oc, *args)


PEER_ROWS = 16
PEER_ECHUNK = 256
PEER_TCHUNK = 256


def _peer_dense_kernel(ht_ref, s1_ref, s2_ref, tau_ref, u_ref, vt_ref, x2_ref, g_ref,
                       y_ref, acc_ref, at_ref, ct_ref, *, eb, tb):
    j = pl.program_id(1)
    n_i1 = eb // PEER_NKEYS

    @pl.when(j == 0)
    def _():
        acc_ref[...] = jnp.zeros_like(acc_ref)

    ec, tcw = PEER_ECHUNK, min(PEER_TCHUNK, tb)
    n_ec, n_tc = eb // ec, tb // tcw

    def pre_act(me, nt):
        at_ref[me * ec:(me + 1) * ec, nt * tcw:(nt + 1) * tcw] = _dot(
            u_ref[me * ec:(me + 1) * ec, :], ht_ref[:, nt * tcw:(nt + 1) * tcw])

    def strip(me, nt):
        for tc in range(nt * tcw // LANES, (nt + 1) * tcw // LANES):
            ls = slice(tc * LANES, (tc + 1) * LANES)
            for rb in range(PEER_NKEYS // PEER_ROWS):
                rs = slice(rb * PEER_ROWS, (rb + 1) * PEER_ROWS)
                iis = range(me * ec // PEER_NKEYS, (me + 1) * ec // PEER_NKEYS)
                w = {ii: jnp.zeros((PEER_ROWS, LANES), F32) for ii in iis}
                for h in range(PEER_HEADS):
                    s2 = s2_ref[h, rs, ls]
                    tau = tau_ref[h:h + 1, ls]
                    for ii in iis:
                        sm = s2 + s1_ref[h, ii:ii + 1, ls]
                        w[ii] = w[ii] + jnp.where(sm >= tau, jnp.exp2(sm), 0.0)
                for ii in iis:
                    rows = slice(ii * PEER_NKEYS + rb * PEER_ROWS, ii * PEER_NKEYS + (rb + 1) * PEER_ROWS)
                    a = at_ref[rows, ls]
                    gelu = 0.5 * a * (1.0 + lax.erf(a * (2.0 ** -0.5)))
                    ct_ref[rows, ls] = (w[ii] * gelu).astype(BF16)

    def project(mo, nt):
        rows = slice(mo * (D_MODEL // n_ec), (mo + 1) * (D_MODEL // n_ec))
        cols = slice(nt * tcw, (nt + 1) * tcw)
        acc_ref[rows, cols] += _dot(vt_ref[rows, :], ct_ref[:, cols])

    order = [(me, nt) for nt in range(n_tc) for me in range(n_ec)]
    pre_act(*order[0])
    for k, (me, nt) in enumerate(order):
        if k + 1 < len(order):
            pre_act(*order[k + 1])
        if nt > 0:
            project(me, nt - 1)
        strip(me, nt)
    for mo in range(n_ec):
        project(mo, n_tc - 1)

    @pl.when(j == pl.num_programs(1) - 1)
    def _():
        y_ref[...] = _rms(x2_ref[...] + acc_ref[...].T, g_ref[...])


def peer_dense(ht, s1, s2, tau, u, vt, x2, g, tb, eb):
    n = x2.shape[0]
    tok = lambda rows: pl.BlockSpec((rows, tb), lambda i, j: (0, i))
    return pl.pallas_call(
        functools.partial(_peer_dense_kernel, eb=eb, tb=tb),
        grid=(n // tb, PEER_N // eb),
        in_specs=[tok(D_MODEL), pl.BlockSpec((PEER_HEADS, eb // PEER_NKEYS, tb), lambda i, j: (0, j, i)),
                  pl.BlockSpec((PEER_HEADS, PEER_NKEYS, tb), lambda i, j: (0, 0, i)), tok(PEER_HEADS),
                  pl.BlockSpec((eb, D_MODEL), lambda i, j: (j, 0)),
                  pl.BlockSpec((D_MODEL, eb), lambda i, j: (0, j)),
                  pl.BlockSpec((tb, D_MODEL), lambda i, j: (i, 0)),
                  pl.BlockSpec((1, D_MODEL), lambda i, j: (0, 0))],
        out_specs=pl.BlockSpec((tb, D_MODEL), lambda i, j: (i, 0)),
        out_shape=jax.ShapeDtypeStruct((n, D_MODEL), F32),
        scratch_shapes=[pltpu.VMEM((D_MODEL, tb), F32), pltpu.VMEM((eb, tb), F32), pltpu.VMEM((eb, tb), BF16)],
        compiler_params=_params("parallel", "arbitrary"),
        name="peer_dense",
    )(ht, s1, s2, tau, u, vt, x2, g.reshape(1, D_MODEL))


def _trunk(x, z, oa, ob, mk, mv, kcol, vcol, batch, tq, w, tb_tok, tb_peer, pad_rows):
    x1, qc = mix_out(x, oa, ob, z, w["beta_a"], w["gnorm_b"], w["w_out"], w["norm_cross"], w["w_cq"], tb_tok)
    if pad_rows:
        n_new = x.shape[0] // batch
        qc = jnp.pad(qc.reshape(batch, n_new, X_WIDTH), ((0, 0), (0, pad_rows - n_new), (0, 0)))
        oc = cross(qc.reshape(batch * pad_rows, X_WIDTH), mk, mv, kcol, vcol, batch, tq)
        oc = oc.reshape(batch, pad_rows, X_WIDTH)[:, :n_new].reshape(x.shape[0], X_WIDTH)
    else:
        oc = cross(qc, mk, mv, kcol, vcol, batch, tq)
    x2, ht, s1, s2, tau = peer_pre(x1, oc, w["w_co"], w["norm_ffn"], w["w_pq_t"], w["k1"], w["k2"], tb_tok)
    return peer_dense(ht, s1, s2, tau, w["u"], w["v_t"], x2, w["norm_final"], tb_peer, 1024)


def kernel(x_prompt, x_sample, cache_swa_k, cache_swa_v, state_hgrn, cache_mem_k, cache_mem_v, mem_prompt, norm_mix, w_in, lb_logits, beta_a, gnorm_b, w_out, norm_cross, norm_mem, w_cq, w_mk, w_mv, w_co, norm_ffn, w_pq, peer_k1, peer_k2, peer_u, peer_v, norm_final):
    batch, seq, _ = x_prompt.shape
    dbatch, dseq, _ = x_sample.shape
    win = cache_swa_k.shape[2]
    rows8 = 8
    layer = 0
    lb = jnp.cumsum(jax.nn.softmax(lb_logits.astype(F32), axis=0), axis=0)[layer].reshape(1, B_WIDTH)
    w = {
        "beta_a": beta_a[layer], "gnorm_b": gnorm_b[layer], "w_out": w_out[layer].astype(BF16),
        "norm_cross": norm_cross[layer], "w_cq": w_cq[layer].astype(BF16), "w_co": w_co[layer].astype(BF16),
        "norm_ffn": norm_ffn[layer], "w_pq_t": w_pq[layer].T.astype(BF16),
        "k1": peer_k1[layer], "k2": peer_k2[layer],
        "u": peer_u[layer].astype(BF16), "v_t": peer_v[layer].T.astype(BF16), "norm_final": norm_final,
    }
    w_in_b = w_in[layer].astype(BF16)
    w_mem = jnp.concatenate([w_mk[layer], w_mv[layer]], axis=1).astype(BF16)

    xp = x_prompt.reshape(batch * seq, D_MODEL)
    zp = norm_matmul(xp, norm_mix[layer], w_in_b, 256)
    oa_p = attn_prompt(zp, batch, seq)
    ob_p, st_p = hgrn(zp, lb, None, batch, seq, 64, 16, 64)
    memkv = norm_matmul(mem_prompt.reshape(batch * N_MEM, D_MODEL), norm_mem[layer], w_mem, 256)
    y_p = _trunk(xp, zp, oa_p, ob_p, memkv, memkv, 0, 1, batch, 512, w, 256, 512, 0)

    xs = x_sample.reshape(dbatch * dseq, D_MODEL)
    zs = norm_matmul(xs, norm_mix[layer], w_in_b, dbatch * dseq)
    zs8 = jnp.pad(zs.reshape(dbatch, dseq, MIX_IN), ((0, 0), (0, rows8 - dseq), (0, 0)))
    oa_s = attn_sample(zs8, cache_swa_k[layer].reshape(dbatch, win, A_WIDTH),
                       cache_swa_v[layer].reshape(dbatch, win, A_WIDTH), dseq)
    oa_s = oa_s[:, :dseq].reshape(dbatch * dseq, A_WIDTH)
    ob_s, st_s = hgrn(zs8.reshape(dbatch * rows8, MIX_IN), lb, state_hgrn[layer], dbatch, rows8, rows8, rows8, dseq)
    ob_s = ob_s.reshape(dbatch, rows8, B_WIDTH)[:, :dseq].reshape(dbatch * dseq, B_WIDTH)
    y_s = _trunk(xs, zs, oa_s, ob_s, cache_mem_k[layer].reshape(dbatch * N_MEM, X_WIDTH),
                 cache_mem_v[layer].reshape(dbatch * N_MEM, X_WIDTH), 0, 0, dbatch, rows8, w,
                 dbatch * dseq, dbatch * dseq, rows8)

    keep = min(BRANCHES[-1][0], seq)
    kp = zp[:, A_WIDTH:2 * A_WIDTH].reshape(batch, seq, A_HEADS, A_HD)[:, seq - keep:]
    vp = zp[:, 2 * A_WIDTH:3 * A_WIDTH].reshape(batch, seq, A_HEADS, A_HD)[:, seq - keep:]
    return (y_p.reshape(batch, seq, D_MODEL), y_s.reshape(dbatch, dseq, D_MODEL),
            kp[None], vp[None], st_p[None],
            memkv[:, :X_WIDTH].reshape(1, batch, N_MEM, X_HEADS, X_HD),
            memkv[:, X_WIDTH:].reshape(1, batch, N_MEM, X_HEADS, X_HD),
            zs[:, A_WIDTH:2 * A_WIDTH].reshape(1, dbatch, dseq, A_HEADS, A_HD),
            zs[:, 2 * A_WIDTH:3 * A_WIDTH].reshape(1, dbatch, dseq, A_HEADS, A_HD),
            st_s[None])
```

```python
import functools

import jax
import jax.numpy as jnp
import numpy as np
from jax import lax
from jax.experimental import pallas as pl
from jax.experimental.pallas import tpu as pltpu

F32 = jnp.float32
BF16 = jnp.bfloat16
EPS = 1e-6
NEG_INF = float("-inf")

LANES = 128
D_MODEL = 1024
A_HEADS, A_HD = 8, 64
A_WIDTH = A_HEADS * A_HD
BRANCHES = ((128, 1), (512, 4), (2048, 16))
SPAN = 128
ATTN_UNROLL = 8
B_HEADS, B_DK, B_DV = 4, 128, 128
B_WIDTH = B_HEADS * B_DV
MIX_IN = 3 * A_WIDTH + 2 * B_HEADS * B_DK + 2 * B_WIDTH
N_MEM = 256
X_HEADS, X_HD = 4, 128
X_WIDTH = X_HEADS * X_HD
PEER_HEADS = 8
PEER_NKEYS = 128
PEER_N = PEER_NKEYS * PEER_NKEYS
PEER_HALF = 128
PEER_TOPK = 16
VMEM_LIMIT = 56 * 1024 * 1024


def _params(*sem):
    return pltpu.CompilerParams(dimension_semantics=sem, vmem_limit_bytes=VMEM_LIMIT)


def _rms(x, g):
    return x * lax.rsqrt(jnp.mean(x * x, axis=-1, keepdims=True) + EPS) * g


def _dot(a, b):
    return jnp.dot(a, b, preferred_element_type=F32)


def _dot_nt(a, b):
    return lax.dot_general(a, b, (((1,), (1,)), ((), ())), preferred_element_type=F32)


def _norm_matmul_kernel(x_ref, g_ref, w_ref, o_ref):
    h = _rms(x_ref[...], g_ref[...])
    o_ref[...] = _dot(h.astype(BF16), w_ref[...])


def norm_matmul(x, g, w, tb):
    n, d = x.shape
    m = w.shape[1]
    return pl.pallas_call(
        _norm_matmul_kernel,
        grid=(n // tb,),
        in_specs=[pl.BlockSpec((tb, d), lambda i: (i, 0)),
                  pl.BlockSpec((1, d), lambda i: (0, 0)),
                  pl.BlockSpec((d, m), lambda i: (0, 0))],
        out_specs=pl.BlockSpec((tb, m), lambda i: (i, 0)),
        out_shape=jax.ShapeDtypeStruct((n, m), F32),
        compiler_params=_params("parallel"),
        name="norm_matmul",
    )(x, g.reshape(1, d), w)


def _attn_prompt_kernel(q_ref, k_ref, v_ref, o_ref, l_ref):
    seq = q_ref.shape[0]
    lane = lax.broadcasted_iota(jnp.int32, (1, LANES), 1)
    head0 = lane < A_HD
    qi = lax.broadcasted_iota(jnp.int32, (SPAN, 2 * SPAN), 0)
    kk = lax.broadcasted_iota(jnp.int32, (SPAN, 2 * SPAN), 1)
    dist = SPAN + qi - kk
    band = (dist >= 0) & (dist <= SPAN)
    scale = A_HD ** -0.5

    for bi, (window, dil) in enumerate(BRANCHES):
        assert window // dil == SPAN
        step = SPAN * dil

        def rows(start, dil=dil):
            return pl.ds(start, SPAN, stride=dil) if dil > 1 else pl.ds(start, SPAN)

        def block(idx, bi=bi, dil=dil, step=step, rows=rows):
            r = idx % dil
            blk = idx // dil
            cur = r + blk * step
            prev = r + jnp.maximum(blk - 1, 0) * step
            q = q_ref[rows(cur), :] * scale
            kc = jnp.concatenate([k_ref[rows(prev), :], k_ref[rows(cur), :]], axis=0).astype(BF16)
            vc = jnp.concatenate([v_ref[rows(prev), :], v_ref[rows(cur), :]], axis=0).astype(BF16)
            old = (o_ref[rows(cur), :], l_ref[rows(cur), :]) if bi > 0 else None
            valid = band & ((kk >= SPAN) | (blk > 0))
            outs, lses = [], []
            for hh in range(2):
                hm = head0 if hh == 0 else jnp.logical_not(head0)
                qh = jnp.where(hm, q, 0.0).astype(BF16)
                s = jnp.where(valid, _dot_nt(qh, kc), NEG_INF)
                m = jnp.max(s, axis=-1, keepdims=True)
                p = jnp.exp(s - m)
                den = jnp.sum(p, axis=-1, keepdims=True)
                outs.append(_dot(p.astype(BF16), vc) * (1.0 / den))
                lses.append(m + jnp.log(den))
            o_new = jnp.where(head0, outs[0], outs[1])
            l_new = jnp.where(head0, lses[0], lses[1])
            if old is not None:
                o_old, l_old = old
                mx = jnp.maximum(l_old, l_new)
                wa = jnp.exp(l_old - mx)
                wb = jnp.exp(l_new - mx)
                tot = wa + wb
                o_new = (wa * o_old + wb * o_new) * (1.0 / tot)
                l_new = mx + jnp.log(tot)
            return cur, o_new, l_new

        def body(g, carry, rows=rows, block=block):
            done = [block(g * ATTN_UNROLL + u) for u in range(ATTN_UNROLL)]
            for cur, o_new, l_new in done:
                o_ref[rows(cur), :] = o_new
                l_ref[rows(cur), :] = l_new
            return carry

        lax.fori_loop(0, seq // SPAN // ATTN_UNROLL, body, 0)


def attn_prompt(z, batch, seq):
    npair = A_WIDTH // LANES
    return pl.pallas_call(
        _attn_prompt_kernel,
        grid=(batch, npair),
        in_specs=[pl.BlockSpec((seq, LANES), lambda b, p: (b, p)),
                  pl.BlockSpec((seq, LANES), lambda b, p: (b, npair + p)),
                  pl.BlockSpec((seq, LANES), lambda b, p: (b, 2 * npair + p))],
        out_specs=pl.BlockSpec((seq, LANES), lambda b, p: (b, p)),
        out_shape=jax.ShapeDtypeStruct((batch * seq, A_WIDTH), F32),
        scratch_shapes=[pltpu.VMEM((seq, LANES), F32)],
        compiler_params=_params("parallel", "parallel"),
        name="attn_prompt",
    )(z, z, z)


def _attn_sample_kernel(q_ref, kn_ref, vn_ref, kc_ref, vc_ref, o_ref, *, n_new):
    win = kc_ref.shape[1]
    rows8 = q_ref.shape[1]
    nrow = A_HEADS * rows8
    q8 = q_ref[0] * (A_HD ** -0.5)
    qe = jnp.concatenate([q8] * A_HEADS, axis=0)
    row = lax.broadcasted_iota(jnp.int32, (nrow, A_WIDTH), 0)
    col = lax.broadcasted_iota(jnp.int32, (nrow, A_WIDTH), 1)
    own = (col // A_HD) == (row // rows8)
    qe = jnp.where(own, qe, 0.0).astype(BF16)

    def multiplicity(delta, ok):
        mult = jnp.zeros(delta.shape, F32)
        for window, dil in BRANCHES:
            hit = ok & (delta >= 0) & ((delta & (dil - 1)) == 0) & (delta <= window)
            mult = mult + hit.astype(F32)
        return mult

    t_c = lax.broadcasted_iota(jnp.int32, (nrow, win), 0) % rows8
    p_c = lax.broadcasted_iota(jnp.int32, (nrow, win), 1)
    mult_c = multiplicity(win + t_c - p_c, p_c >= 0)
    t_n = lax.broadcasted_iota(jnp.int32, (nrow, rows8), 0) % rows8
    u_n = lax.broadcasted_iota(jnp.int32, (nrow, rows8), 1)
    mult_n = multiplicity(t_n - u_n, u_n < n_new)

    s_c = jnp.where(mult_c > 0, _dot_nt(qe, kc_ref[0].astype(BF16)), NEG_INF)
    s_n = jnp.where(mult_n > 0, _dot_nt(qe, kn_ref[0].astype(BF16)), NEG_INF)
    m = jnp.maximum(jnp.max(s_c, axis=-1, keepdims=True), jnp.max(s_n, axis=-1, keepdims=True))
    e_c = mult_c * jnp.exp(s_c - m)
    e_n = mult_n * jnp.exp(s_n - m)
    den = jnp.sum(e_c, axis=-1, keepdims=True) + jnp.sum(e_n, axis=-1, keepdims=True)
    o = (_dot(e_c.astype(BF16), vc_ref[0].astype(BF16)) + _dot(e_n.astype(BF16), vn_ref[0].astype(BF16))) / den
    o = jnp.where(own, o, 0.0)
    acc = o[0:rows8]
    for h in range(1, A_HEADS):
        acc = acc + o[h * rows8:(h + 1) * rows8]
    o_ref[0] = acc


def attn_sample(z8, cache_k, cache_v, n_new):
    nb, rows8, _ = z8.shape
    win = cache_k.shape[1]
    return pl.pallas_call(
        functools.partial(_attn_sample_kernel, n_new=n_new),
        grid=(nb,),
        in_specs=[pl.BlockSpec((1, rows8, A_WIDTH), lambda b: (b, 0, 0)),
                  pl.BlockSpec((1, rows8, A_WIDTH), lambda b: (b, 0, 1)),
                  pl.BlockSpec((1, rows8, A_WIDTH), lambda b: (b, 0, 2)),
                  pl.BlockSpec((1, win, A_WIDTH), lambda b: (b, 0, 0)),
                  pl.BlockSpec((1, win, A_WIDTH), lambda b: (b, 0, 0))],
        out_specs=pl.BlockSpec((1, rows8, A_WIDTH), lambda b: (b, 0, 0)),
        out_shape=jax.ShapeDtypeStruct((nb, rows8, A_WIDTH), F32),
        compiler_params=_params("parallel"),
        name="attn_sample",
    )(z8, z8, z8, cache_k, cache_v)


def _hgrn_kernel(q_ref, f_ref, i_ref, lb_ref, sel_ref, s0_ref, o_ref, sn_ref, st_ref, *, chunk, sub, n_valid, has_s0):
    c_idx = pl.program_id(1)

    @pl.when(c_idx == 0)
    def _():
        for h in range(B_HEADS):
            st_ref[h] = s0_ref[0, h].T if has_s0 else jnp.zeros((B_DV, B_DK), F32)

    lb = lb_ref[...]
    f = lb + (1.0 - lb) * jax.nn.sigmoid(f_ref[...])
    logf = jnp.log(f)
    kgate = 1.0 - f
    q = jax.nn.silu(q_ref[...])
    if n_valid < chunk:
        live = lax.broadcasted_iota(jnp.int32, (chunk, 1), 0) < n_valid
        logf = jnp.where(live, logf, 0.0)
        kgate = jnp.where(live, kgate, 0.0)
        q = jnp.where(live, q, 0.0)
    iv = i_ref[...]
    tr = lax.broadcasted_iota(jnp.int32, (chunk, chunk), 0)
    tc = lax.broadcasted_iota(jnp.int32, (chunk, chunk), 1)
    tril = (tr >= tc).astype(F32)
    cum = jnp.dot(tril, logf, preferred_element_type=F32, precision=lax.Precision.HIGHEST)

    nsub = chunk // sub
    srow = lax.broadcasted_iota(jnp.int32, (sub, B_DK), 0)
    krow = lax.broadcasted_iota(jnp.int32, (chunk, 1), 0)
    same_sub = (tr // sub) == (tc // sub)
    lane_sum = sel_ref[...]

    for h in range(B_HEADS):
        sl = slice(h * B_DK, (h + 1) * B_DK)
        c_h, q_h, k_h, i_h = cum[:, sl], q[:, sl], kgate[:, sl], iv[:, sl]
        st = st_ref[h]
        i_b = i_h.astype(BF16)
        o_inter = _dot_nt((q_h * jnp.exp(c_h)).astype(BF16), st.astype(BF16))
        diag_rows, off_rows = [], []
        for blk in range(nsub):
            r0 = blk * sub
            c_i, q_i, k_i = c_h[r0:r0 + sub], q_h[r0:r0 + sub], k_h[r0:r0 + sub]
            parts = []
            for s in range(sub):
                dec = jnp.exp(jnp.where(srow >= s, c_i - c_i[s:s + 1], NEG_INF))
                parts.append((q_i * k_i[s:s + 1] * dec).astype(BF16))
            diag_rows.append(jnp.concatenate(parts, axis=1))
            if blk > 0:
                c_ref = c_h[r0 - 1:r0]
                q_t = (q_i * jnp.exp(c_i - c_ref)).astype(BF16)
                k_t = jnp.where(krow < r0, k_h * jnp.exp(jnp.minimum(c_ref - c_h, 0.0)), 0.0).astype(BF16)
                off_rows.append(_dot_nt(q_t, k_t))
            else:
                off_rows.append(jnp.zeros((sub, chunk), F32))
        cat = (lambda xs: xs[0] if nsub == 1 else jnp.concatenate(xs, axis=0))
        a_full = jnp.where(same_sub, _dot(cat(diag_rows), lane_sum), 0.0) + cat(off_rows)
        o_ref[:, sl] = o_inter + _dot(a_full.astype(BF16), i_b)
        c_last = c_h[chunk - 1:chunk]
        k_dec = (k_h * jnp.exp(c_last - c_h)).astype(BF16)
        st_ref[h] = st * jnp.exp(c_last) + _dot(i_h.T.astype(BF16), k_dec)

    @pl.when(c_idx == pl.num_programs(1) - 1)
    def _():
        for h in range(B_HEADS):
            sn_ref[0, h] = st_ref[h].T


def hgrn(z, lb, s0, batch, seq, chunk, sub, n_valid):
    nchunk = seq // chunk
    qcol = 3 * A_WIDTH // B_WIDTH
    has_s0 = s0 is not None
    if s0 is None:
        s0 = jnp.zeros((1, B_HEADS, B_DK, B_DV), F32)
    s0_map = (lambda b, c: (b, 0, 0, 0)) if has_s0 else (lambda b, c: (0, 0, 0, 0))
    tok = lambda col: pl.BlockSpec((chunk, B_WIDTH), lambda b, c, col=col: (b * nchunk + c, col))
    lane_sum = (np.arange(sub * B_DK)[:, None] // B_DK == np.arange(chunk)[None, :] % sub)
    lane_sum = jnp.asarray(lane_sum, dtype=BF16)
    return pl.pallas_call(
        functools.partial(_hgrn_kernel, chunk=chunk, sub=sub, n_valid=n_valid, has_s0=has_s0),
        grid=(batch, nchunk),
        in_specs=[tok(qcol), tok(qcol + 1), tok(qcol + 2),
                  pl.BlockSpec((1, B_WIDTH), lambda b, c: (0, 0)),
                  pl.BlockSpec((sub * B_DK, chunk), lambda b, c: (0, 0)),
                  pl.BlockSpec((1, B_HEADS, B_DK, B_DV), s0_map)],
        out_specs=[pl.BlockSpec((chunk, B_WIDTH), lambda b, c: (b * nchunk + c, 0)),
                   pl.BlockSpec((1, B_HEADS, B_DK, B_DV), lambda b, c: (b, 0, 0, 0))],
        out_shape=[jax.ShapeDtypeStruct((batch * seq, B_WIDTH), F32),
                   jax.ShapeDtypeStruct((batch, B_HEADS, B_DK, B_DV), F32)],
        scratch_shapes=[pltpu.VMEM((B_HEADS, B_DV, B_DK), F32)],
        compiler_params=_params("parallel", "arbitrary"),
        name="hgrn",
    )(z, z, z, lb, lane_sum, s0)


def _mix_out_kernel(x_ref, oa_ref, ob_ref, gb_ref, beta_ref, gn_ref, wout_ref, nc_ref, wcq_ref, x1_ref, qc_ref):
    parts = [_rms(oa_ref[...], beta_ref[...]).astype(BF16)]
    ob = ob_ref[...]
    gate = jax.nn.silu(gb_ref[...])
    gn = gn_ref[...]
    for h in range(B_HEADS):
        sl = slice(h * B_DV, (h + 1) * B_DV)
        parts.append((_rms(ob[:, sl], gn[:, sl]) * gate[:, sl]).astype(BF16))
    x1 = x_ref[...] + _dot(jnp.concatenate(parts, axis=-1), wout_ref[...])
    x1_ref[...] = x1
    qc_ref[...] = _dot(_rms(x1, nc_ref[...]).astype(BF16), wcq_ref[...])


def mix_out(x, oa, ob, z, beta, gn, wout, nc, wcq, tb):
    n = x.shape[0]
    gcol = MIX_IN // B_WIDTH - 1
    row = lambda w: pl.BlockSpec((tb, w), lambda i: (i, 0))
    const = lambda a: pl.BlockSpec(a.shape, lambda i: (0,) * a.ndim)
    args = (beta.reshape(1, A_WIDTH), gn.reshape(1, B_WIDTH), wout, nc.reshape(1, D_MODEL), wcq)
    return pl.pallas_call(
        _mix_out_kernel,
        grid=(n // tb,),
        in_specs=[row(D_MODEL), row(A_WIDTH), row(B_WIDTH), pl.BlockSpec((tb, B_WIDTH), lambda i: (i, gcol))]
        + [const(a) for a in args],
        out_specs=[row(D_MODEL), row(X_WIDTH)],
        out_shape=[jax.ShapeDtypeStruct((n, D_MODEL), F32), jax.ShapeDtypeStruct((n, X_WIDTH), F32)],
        compiler_params=_params("parallel"),
        name="mix_out",
    )(x, oa, ob, z, *args)


def _cross_kernel(q_ref, k_ref, v_ref, o_ref):
    scale = X_HD ** -0.5
    for h in range(X_HEADS):
        sl = slice(h * X_HD, (h + 1) * X_HD)
        s = _dot_nt(q_ref[:, sl].astype(BF16), k_ref[:, sl].astype(BF16)) * scale
        m = jnp.max(s, axis=-1, keepdims=True)
        p = jnp.exp(s - m)
        den = jnp.sum(p, axis=-1, keepdims=True)
        o_ref[:, sl] = _dot(p.astype(BF16), v_ref[:, sl].astype(BF16)) / den


def cross(qc, mk, mv, kcol, vcol, batch, tq):
    n = qc.shape[0]
    per = n // batch // tq
    return pl.pallas_call(
        _cross_kernel,
        grid=(batch, per),
        in_specs=[pl.BlockSpec((tq, X_WIDTH), lambda b, i: (b * per + i, 0)),
                  pl.BlockSpec((N_MEM, X_WIDTH), lambda b, i: (b, kcol)),
                  pl.BlockSpec((N_MEM, X_WIDTH), lambda b, i: (b, vcol))],
        out_specs=pl.BlockSpec((tq, X_WIDTH), lambda b, i: (b * per + i, 0)),
        out_shape=jax.ShapeDtypeStruct((n, X_WIDTH), F32),
        compiler_params=_params("parallel", "parallel"),
        name="cross",
    )(qc, mk, mv)


def _top_values(s, k):
    vals = []
    for _ in range(k):
        m = jnp.max(s, axis=0, keepdims=True)
        vals.append(m)
        s = jnp.where(s == m, NEG_INF, s)
    return jnp.concatenate(vals, axis=0)


def _merge_sort_pairs(n):
    pairs = []
    p = 1
    while p < n:
        k = p
        while k >= 1:
            for j in range(k % p, n - k, 2 * k):
                for i in range(min(k, n - j - k)):
                    if (i + j) // (2 * p) == (i + j + k) // (2 * p):
                        pairs.append((i + j, i + j + k))
            k //= 2
        p *= 2
    return pairs


def _sort_desc(xs, pairs):
    xs = list(xs)
    for i, j in pairs:
        xs[i], xs[j] = jnp.maximum(xs[i], xs[j]), jnp.minimum(xs[i], xs[j])
    return xs


def _top16_sorted(s):
    n = PEER_TOPK
    sub = s.shape[0] // n
    xs = _sort_desc([s[r * sub:(r + 1) * sub] for r in range(n)], _merge_sort_pairs(n))
    bitonic = [(i, i + d) for d in (8, 4, 2, 1) for i in range(n) if (i // d) % 2 == 0]
    for shift in (4, 2, 1):
        ys = [pltpu.roll(x, shift, axis=0) for x in xs]
        xs = _sort_desc([jnp.maximum(xs[i], ys[n - 1 - i]) for i in range(n)], bitonic)
    return xs


def _rows(slabs):
    row = lax.broadcasted_iota(jnp.int32, slabs[0].shape, 0)
    out = slabs[-1]
    for i in range(len(slabs) - 2, -1, -1):
        out = jnp.where(row == i, slabs[i], out)
    return out


def _pair_sums(x1, v1_hi, x2_0, v2_lo, v2_hi):
    half = PEER_TOPK // 2
    row = lax.broadcasted_iota(jnp.int32, (half, 1), 0)
    parts = [x1[0] + v2_lo, x1[0] + v2_hi]
    for a in range(1, half):
        parts.append(jnp.where(row < PEER_TOPK // (a + 1), x1[a] + v2_lo, NEG_INF))
    parts.append(v1_hi + x2_0)
    return jnp.concatenate(parts, axis=0)


def _peer_pre_kernel(x1_ref, oc_ref, wco_ref, nf_ref, wpq_ref, k1_ref, k2_ref,
                     x2_ref, ht_ref, s1_ref, s2_ref, tau_ref):
    x2 = x1_ref[...] + _dot(oc_ref[...].astype(BF16), wco_ref[...])
    x2_ref[...] = x2
    ht = _rms(x2, nf_ref[...]).T.astype(BF16)
    ht_ref[...] = ht
    qt = _dot(wpq_ref[...], ht)
    log2e = float(np.log2(np.e))

    def scores(keys3, q):
        q_hi = q.astype(BF16)
        q_lo = (q - q_hi.astype(F32)).astype(BF16)
        return _dot(keys3, jnp.concatenate([q_hi, q_lo, q_hi], axis=0)) * log2e

    for h in range(PEER_HEADS):
        r0 = h * 2 * PEER_HALF
        s1 = scores(k1_ref[h], qt[r0:r0 + PEER_HALF])
        s2 = scores(k2_ref[h], qt[r0 + PEER_HALF:r0 + 2 * PEER_HALF])
        x1 = _top16_sorted(s1)
        x2 = _top16_sorted(s2)
        half = PEER_TOPK // 2
        v1_hi, v2_lo, v2_hi = _rows(x1[half:]), _rows(x2[:half]), _rows(x2[half:])
        cand = _pair_sums(x1, v1_hi, x2[0], v2_lo, v2_hi)
        sc = _top_values(cand, PEER_TOPK)
        top = sc[0:1]
        shift = top + jnp.log2(jnp.sum(jnp.exp2(sc - top), axis=0, keepdims=True))
        picked = cand >= sc[PEER_TOPK - 1:PEER_TOPK]
        shifted = _pair_sums(x1, v1_hi, x2[0] - shift, v2_lo - shift, v2_hi - shift)
        tau = jnp.min(jnp.where(picked, shifted, jnp.inf), axis=0, keepdims=True)
        s1_ref[h] = s1
        s2_ref[h] = s2 - shift
        tau_ref[h:h + 1, :] = tau


def peer_pre(x1, oc, wco, nf, wpq_t, k1, k2, tb):
    n = x1.shape[0]
    const = lambda a: pl.BlockSpec(a.shape, lambda i: (0,) * a.ndim)
    args = (wco, nf.reshape(1, D_MODEL), wpq_t, k1, k2)
    return pl.pallas_call(
        _peer_pre_kernel,
        grid=(n // tb,),
        in_specs=[pl.BlockSpec((tb, D_MODEL), lambda i: (i, 0)), pl.BlockSpec((tb, X_WIDTH), lambda i: (i, 0))]
        + [const(a) for a in args],
        out_specs=[pl.BlockSpec((tb, D_MODEL), lambda i: (i, 0)),
                   pl.BlockSpec((D_MODEL, tb), lambda i: (0, i)),
                   pl.BlockSpec((PEER_HEADS, PEER_NKEYS, tb), lambda i: (0, 0, i)),
                   pl.BlockSpec((PEER_HEADS, PEER_NKEYS, tb), lambda i: (0, 0, i)),
                   pl.BlockSpec((PEER_HEADS, tb), lambda i: (0, i))],
        out_shape=[jax.ShapeDtypeStruct((n, D_MODEL), F32),
                   jax.ShapeDtypeStruct((D_MODEL, n), BF16),
                   jax.ShapeDtypeStruct((PEER_HEADS, PEER_NKEYS, n), F32),
                   jax.ShapeDtypeStruct((PEER_HEADS, PEER_NKEYS, n), F32),
                   jax.ShapeDtypeStruct((PEER_HEADS, n), F32)],
        compiler_params=_params("parallel"),
        name="peer_pre",
    )(x1, oc, *args)


PEER_ROWS = 16
PEER_EBLOCK = 1024
PEER_ECHUNK = 256
PEER_TCHUNK = 256


def _peer_dense_kernel(ht_ref, s1_ref, s2_ref, tau_ref, u_ref, vt_ref, x2_ref, g_ref,
                       y_ref, acc_ref, *pieces, eb, tb):
    j = pl.program_id(1)

    @pl.when(j == 0)
    def _():
        acc_ref[...] = jnp.zeros_like(acc_ref)

    ec, tcw = PEER_ECHUNK, min(PEER_TCHUNK, tb)
    n_ec, n_tc = eb // ec, tb // tcw
    at_refs = [pieces[me * n_tc:(me + 1) * n_tc] for me in range(n_ec)]
    ct_refs = pieces[n_ec * n_tc:]

    def pre_act(me, nt):
        at_refs[me][nt][...] = _dot(u_ref[me * ec:(me + 1) * ec, :], ht_ref[:, nt * tcw:(nt + 1) * tcw])

    def strip(me, nt):
        n_loc = ec // PEER_NKEYS
        for tc in range(tcw // LANES):
            ls = slice(tc * LANES, (tc + 1) * LANES)
            gs = slice(nt * tcw + tc * LANES, nt * tcw + (tc + 1) * LANES)
            for rb in range(PEER_NKEYS // PEER_ROWS):
                rs = slice(rb * PEER_ROWS, (rb + 1) * PEER_ROWS)
                w = [jnp.zeros((PEER_ROWS, LANES), F32) for _ in range(n_loc)]
                for h in range(PEER_HEADS):
                    s2 = s2_ref[h, rs, gs]
                    tau = tau_ref[h:h + 1, gs]
                    for il in range(n_loc):
                        ii = me * n_loc + il
                        sm = s2 + s1_ref[h, ii:ii + 1, gs]
                        w[il] = w[il] + jnp.where(sm >= tau, jnp.exp2(sm), 0.0)
                for il in range(n_loc):
                    r0 = il * PEER_NKEYS + rb * PEER_ROWS
                    a = at_refs[me][nt][r0:r0 + PEER_ROWS, ls]
                    gelu = 0.5 * a * (1.0 + lax.erf(a * (2.0 ** -0.5)))
                    ct_refs[nt][me * ec + r0:me * ec + r0 + PEER_ROWS, ls] = (w[il] * gelu).astype(BF16)

    def project(mo, nt):
        rows = slice(mo * (D_MODEL // n_ec), (mo + 1) * (D_MODEL // n_ec))
        acc_ref[rows, nt * tcw:(nt + 1) * tcw] += _dot(vt_ref[0, rows, :], ct_refs[nt][...])

    order = [(me, nt) for nt in range(n_tc) for me in range(n_ec)]
    pre_act(*order[0])
    for k, (me, nt) in enumerate(order):
        if k + 1 < len(order):
            pre_act(*order[k + 1])
        if nt > 0:
            project(me, nt - 1)
        strip(me, nt)
    for mo in range(n_ec):
        project(mo, n_tc - 1)

    @pl.when(j == pl.num_programs(1) - 1)
    def _():
        y_ref[...] = _rms(x2_ref[...] + acc_ref[...].T, g_ref[...])


def peer_dense(ht, s1, s2, tau, u, vt, x2, g, tb, eb):
    n = x2.shape[0]
    tok = lambda rows: pl.BlockSpec((rows, tb), lambda i, j: (0, i))
    tcw = min(PEER_TCHUNK, tb)
    pieces = ([pltpu.VMEM((PEER_ECHUNK, tcw), F32)] * ((eb // PEER_ECHUNK) * (tb // tcw))
              + [pltpu.VMEM((eb, tcw), BF16)] * (tb // tcw))
    return pl.pallas_call(
        functools.partial(_peer_dense_kernel, eb=eb, tb=tb),
        grid=(n // tb, PEER_N // eb),
        in_specs=[tok(D_MODEL), pl.BlockSpec((PEER_HEADS, eb // PEER_NKEYS, tb), lambda i, j: (0, j, i)),
                  pl.BlockSpec((PEER_HEADS, PEER_NKEYS, tb), lambda i, j: (0, 0, i)), tok(PEER_HEADS),
                  pl.BlockSpec((eb, D_MODEL), lambda i, j: (j, 0)),
                  pl.BlockSpec((1, D_MODEL, eb), lambda i, j: (j, 0, 0)),
                  pl.BlockSpec((tb, D_MODEL), lambda i, j: (i, 0)),
                  pl.BlockSpec((1, D_MODEL), lambda i, j: (0, 0))],
        out_specs=pl.BlockSpec((tb, D_MODEL), lambda i, j: (i, 0)),
        out_shape=jax.ShapeDtypeStruct((n, D_MODEL), F32),
        scratch_shapes=[pltpu.VMEM((D_MODEL, tb), F32)] + pieces,
        compiler_params=_params("parallel", "arbitrary"),
        name="peer_dense",
    )(ht, s1, s2, tau, u, vt, x2, g.reshape(1, D_MODEL))


def _split3(k):
    hi = k.astype(BF16)
    lo = (k - hi.astype(F32)).astype(BF16)
    return jnp.concatenate([hi, hi, lo], axis=-1)


def _trunk(x, z, oa, ob, mk, mv, kcol, vcol, batch, tq, w, tb_tok, tb_peer, pad_rows):
    x1, qc = mix_out(x, oa, ob, z, w["beta_a"], w["gnorm_b"], w["w_out"], w["norm_cross"], w["w_cq"], tb_tok)
    if pad_rows:
        n_new = x.shape[0] // batch
        qc = jnp.pad(qc.reshape(batch, n_new, X_WIDTH), ((0, 0), (0, pad_rows - n_new), (0, 0)))
        oc = cross(qc.reshape(batch * pad_rows, X_WIDTH), mk, mv, kcol, vcol, batch, tq)
        oc = oc.reshape(batch, pad_rows, X_WIDTH)[:, :n_new].reshape(x.shape[0], X_WIDTH)
    else:
        oc = cross(qc, mk, mv, kcol, vcol, batch, tq)
    x2, ht, s1, s2, tau = peer_pre(x1, oc, w["w_co"], w["norm_ffn"], w["w_pq_t"], w["k1"], w["k2"], tb_tok)
    return peer_dense(ht, s1, s2, tau, w["u"], w["v_blocks"], x2, w["norm_final"], tb_peer, PEER_EBLOCK)


def kernel(x_prompt, x_sample, cache_swa_k, cache_swa_v, state_hgrn, cache_mem_k, cache_mem_v, mem_prompt, norm_mix, w_in, lb_logits, beta_a, gnorm_b, w_out, norm_cross, norm_mem, w_cq, w_mk, w_mv, w_co, norm_ffn, w_pq, peer_k1, peer_k2, peer_u, peer_v, norm_final):
    batch, seq, _ = x_prompt.shape
    dbatch, dseq, _ = x_sample.shape
    win = cache_swa_k.shape[2]
    rows8 = 8
    layer = 0
    lb = jnp.cumsum(jax.nn.softmax(lb_logits.astype(F32), axis=0), axis=0)[layer].reshape(1, B_WIDTH)
    w = {
        "beta_a": beta_a[layer], "gnorm_b": gnorm_b[layer], "w_out": w_out[layer].astype(BF16),
        "norm_cross": norm_cross[layer], "w_cq": w_cq[layer].astype(BF16), "w_co": w_co[layer].astype(BF16),
        "norm_ffn": norm_ffn[layer], "w_pq_t": w_pq[layer].T.astype(BF16),
        "k1": _split3(peer_k1[layer]), "k2": _split3(peer_k2[layer]),
        "u": peer_u[layer].astype(BF16), "norm_final": norm_final,
        "v_blocks": peer_v[layer].astype(BF16).reshape(PEER_N // PEER_EBLOCK, PEER_EBLOCK, D_MODEL).transpose(0, 2, 1),
    }
    w_in_b = w_in[layer].astype(BF16)
    w_mem = jnp.concatenate([w_mk[layer], w_mv[layer]], axis=1).astype(BF16)

    xp = x_prompt.reshape(batch * seq, D_MODEL)
    zp = norm_matmul(xp, norm_mix[layer], w_in_b, 256)
    oa_p = attn_prompt(zp, batch, seq)
    ob_p, st_p = hgrn(zp, lb, None, batch, seq, 64, 16, 64)
    memkv = norm_matmul(mem_prompt.reshape(batch * N_MEM, D_MODEL), norm_mem[layer], w_mem, 256)
    y_p = _trunk(xp, zp, oa_p, ob_p, memkv, memkv, 0, 1, batch, 512, w, 256, 1024, 0)

    xs = x_sample.reshape(dbatch * dseq, D_MODEL)
    zs = norm_matmul(xs, norm_mix[layer], w_in_b, dbatch * dseq)
    zs8 = jnp.pad(zs.reshape(dbatch, dseq, MIX_IN), ((0, 0), (0, rows8 - dseq), (0, 0)))
    oa_s = attn_sample(zs8, cache_swa_k[layer].reshape(dbatch, win, A_WIDTH),
                       cache_swa_v[layer].reshape(dbatch, win, A_WIDTH), dseq)
    oa_s = oa_s[:, :dseq].reshape(dbatch * dseq, A_WIDTH)
    ob_s, st_s = hgrn(zs8.reshape(dbatch * rows8, MIX_IN), lb, state_hgrn[layer], dbatch, rows8, rows8, rows8, dseq)
    ob_s = ob_s.reshape(dbatch, rows8, B_WIDTH)[:, :dseq].reshape(dbatch * dseq, B_WIDTH)
    y_s = _trunk(xs, zs, oa_s, ob_s, cache_mem_k[layer].reshape(dbatch * N_MEM, X_WIDTH),
                 cache_mem_v[layer].reshape(dbatch * N_MEM, X_WIDTH), 0, 0, dbatch, rows8, w,
                 dbatch * dseq, dbatch * dseq, rows8)

    keep = min(BRANCHES[-1][0], seq)
    kp = zp[:, A_WIDTH:2 * A_WIDTH].reshape(batch, seq, A_HEADS, A_HD)[:, seq - keep:]
    vp = zp[:, 2 * A_WIDTH:3 * A_WIDTH].reshape(batch, seq, A_HEADS, A_HD)[:, seq - keep:]
    return (y_p.reshape(batch, seq, D_MODEL), y_s.reshape(dbatch, dseq, D_MODEL),
            kp[None], vp[None], st_p[None],
            memkv[:, :X_WIDTH].reshape(1, batch, N_MEM, X_HEADS, X_HD),
            memkv[:, X_WIDTH:].reshape(1, batch, N_MEM, X_HEADS, X_HD),
            zs[:, A_WIDTH:2 * A_WIDTH].reshape(1, dbatch, dseq, A_HEADS, A_HD),
            zs[:, 2 * A_WIDTH:3 * A_WIDTH].reshape(1, dbatch, dseq, A_HEADS, A_HD),
            st_s[None])
```

```python
import functools

import jax
import jax.numpy as jnp
import numpy as np
from jax import lax
from jax.experimental import pallas as pl
from jax.experimental.pallas import tpu as pltpu

F32 = jnp.float32
BF16 = jnp.bfloat16
EPS = 1e-6
NEG_INF = float("-inf")

LANES = 128
D_MODEL = 1024
A_HEADS, A_HD = 8, 64
A_WIDTH = A_HEADS * A_HD
BRANCHES = ((128, 1), (512, 4), (2048, 16))
SPAN = 128
ATTN_UNROLL = 8
HGRN_GROUP = 4
B_HEADS, B_DK, B_DV = 4, 128, 128
B_WIDTH = B_HEADS * B_DV
MIX_IN = 3 * A_WIDTH + 2 * B_HEADS * B_DK + 2 * B_WIDTH
N_MEM = 256
X_HEADS, X_HD = 4, 128
X_WIDTH = X_HEADS * X_HD
PEER_HEADS = 8
PEER_NKEYS = 128
PEER_N = PEER_NKEYS * PEER_NKEYS
PEER_HALF = 128
PEER_TOPK = 16
VMEM_LIMIT = 56 * 1024 * 1024


def _params(*sem):
    return pltpu.CompilerParams(dimension_semantics=sem, vmem_limit_bytes=VMEM_LIMIT)


def _rms(x, g):
    return x * lax.rsqrt(jnp.mean(x * x, axis=-1, keepdims=True) + EPS) * g


def _dot(a, b):
    return jnp.dot(a, b, preferred_element_type=F32)


def _dot_nt(a, b):
    return lax.dot_general(a, b, (((1,), (1,)), ((), ())), preferred_element_type=F32)


def _norm_matmul_kernel(x_ref, g_ref, w_ref, o_ref):
    h = _rms(x_ref[...], g_ref[...])
    o_ref[...] = _dot(h.astype(BF16), w_ref[...])


def norm_matmul(x, g, w, tb):
    n, d = x.shape
    m = w.shape[1]
    return pl.pallas_call(
        _norm_matmul_kernel,
        grid=(n // tb,),
        in_specs=[pl.BlockSpec((tb, d), lambda i: (i, 0)),
                  pl.BlockSpec((1, d), lambda i: (0, 0)),
                  pl.BlockSpec((d, m), lambda i: (0, 0))],
        out_specs=pl.BlockSpec((tb, m), lambda i: (i, 0)),
        out_shape=jax.ShapeDtypeStruct((n, m), F32),
        compiler_params=_params("parallel"),
        name="norm_matmul",
    )(x, g.reshape(1, d), w)


def _window_rows_kernel(x_ref, o_ref):
    o_ref[0, 0] = x_ref[...].T


def window_rows(z, batch, seq, keep):
    npair = A_WIDTH // LANES
    last = seq // keep - 1
    return pl.pallas_call(
        _window_rows_kernel,
        grid=(2, batch, npair),
        in_specs=[pl.BlockSpec((keep, LANES), lambda kv, b, p: (b * (seq // keep) + last, (1 + kv) * npair + p))],
        out_specs=pl.BlockSpec((1, 1, LANES, keep), lambda kv, b, p: (kv, b, p, 0)),
        out_shape=jax.ShapeDtypeStruct((2, batch, A_WIDTH, keep), F32),
        compiler_params=_params("parallel", "parallel", "parallel"),
        name="window_rows",
    )(z)


def _attn_prompt_kernel(q_ref, k_ref, v_ref, o_ref, l_ref):
    seq = q_ref.shape[0]
    lane = lax.broadcasted_iota(jnp.int32, (1, LANES), 1)
    head0 = lane < A_HD
    qi = lax.broadcasted_iota(jnp.int32, (SPAN, 2 * SPAN), 0)
    kk = lax.broadcasted_iota(jnp.int32, (SPAN, 2 * SPAN), 1)
    dist = SPAN + qi - kk
    band = (dist >= 0) & (dist <= SPAN)
    scale = A_HD ** -0.5

    for bi, (window, dil) in enumerate(BRANCHES):
        assert window // dil == SPAN
        step = SPAN * dil

        def rows(start, dil=dil):
            return pl.ds(start, SPAN, stride=dil) if dil > 1 else pl.ds(start, SPAN)

        def block(idx, bi=bi, dil=dil, step=step, rows=rows):
            r = idx % dil
            blk = idx // dil
            cur = r + blk * step
            prev = r + jnp.maximum(blk - 1, 0) * step
            q = q_ref[rows(cur), :] * scale
            kc = jnp.concatenate([k_ref[rows(prev), :], k_ref[rows(cur), :]], axis=0).astype(BF16)
            vc = jnp.concatenate([v_ref[rows(prev), :], v_ref[rows(cur), :]], axis=0).astype(BF16)
            old = (o_ref[rows(cur), :], l_ref[rows(cur), :]) if bi > 0 else None
            valid = band & ((kk >= SPAN) | (blk > 0))
            outs, lses = [], []
            for hh in range(2):
                hm = head0 if hh == 0 else jnp.logical_not(head0)
                qh = jnp.where(hm, q, 0.0).astype(BF16)
                s = jnp.where(valid, _dot_nt(qh, kc), NEG_INF)
                m = jnp.max(s, axis=-1, keepdims=True)
                p = jnp.exp(s - m)
                den = jnp.sum(p, axis=-1, keepdims=True)
                outs.append(_dot(p.astype(BF16), vc) * (1.0 / den))
                lses.append(m + jnp.log(den))
            o_new = jnp.where(head0, outs[0], outs[1])
            l_new = jnp.where(head0, lses[0], lses[1])
            if old is not None:
                o_old, l_old = old
                mx = jnp.maximum(l_old, l_new)
                wa = jnp.exp(l_old - mx)
                wb = jnp.exp(l_new - mx)
                tot = wa + wb
                o_new = (wa * o_old + wb * o_new) * (1.0 / tot)
                l_new = mx + jnp.log(tot)
            return cur, o_new, l_new

        def body(g, carry, rows=rows, block=block):
            done = [block(g * ATTN_UNROLL + u) for u in range(ATTN_UNROLL)]
            for cur, o_new, l_new in done:
                o_ref[rows(cur), :] = o_new
                l_ref[rows(cur), :] = l_new
            return carry

        lax.fori_loop(0, seq // SPAN // ATTN_UNROLL, body, 0)


def attn_prompt(z, batch, seq):
    npair = A_WIDTH // LANES
    return pl.pallas_call(
        _attn_prompt_kernel,
        grid=(batch, npair),
        in_specs=[pl.BlockSpec((seq, LANES), lambda b, p: (b, p)),
                  pl.BlockSpec((seq, LANES), lambda b, p: (b, npair + p)),
                  pl.BlockSpec((seq, LANES), lambda b, p: (b, 2 * npair + p))],
        out_specs=pl.BlockSpec((seq, LANES), lambda b, p: (b, p)),
        out_shape=jax.ShapeDtypeStruct((batch * seq, A_WIDTH), F32),
        scratch_shapes=[pltpu.VMEM((seq, LANES), F32)],
        compiler_params=_params("parallel", "parallel"),
        name="attn_prompt",
    )(z, z, z)


def _attn_sample_kernel(q_ref, kn_ref, vn_ref, kc_ref, vc_ref, o_ref, *, n_new):
    win = kc_ref.shape[2]
    rows8 = q_ref.shape[1]
    nrow = A_HEADS * rows8
    q8 = q_ref[0] * (A_HD ** -0.5)
    qe = jnp.concatenate([q8] * A_HEADS, axis=0)
    row = lax.broadcasted_iota(jnp.int32, (nrow, A_WIDTH), 0)
    col = lax.broadcasted_iota(jnp.int32, (nrow, A_WIDTH), 1)
    own = (col // A_HD) == (row // rows8)
    qe = jnp.where(own, qe, 0.0).astype(BF16)

    def multiplicity(delta, ok):
        mult = jnp.zeros(delta.shape, F32)
        for window, dil in BRANCHES:
            hit = ok & (delta >= 0) & ((delta & (dil - 1)) == 0) & (delta <= window)
            mult = mult + hit.astype(F32)
        return mult

    t_c = lax.broadcasted_iota(jnp.int32, (nrow, win), 0) % rows8
    p_c = lax.broadcasted_iota(jnp.int32, (nrow, win), 1)
    mult_c = multiplicity(win + t_c - p_c, p_c >= 0)
    t_n = lax.broadcasted_iota(jnp.int32, (nrow, rows8), 0) % rows8
    u_n = lax.broadcasted_iota(jnp.int32, (nrow, rows8), 1)
    mult_n = multiplicity(t_n - u_n, u_n < n_new)

    s_c = jnp.where(mult_c > 0, _dot(qe, kc_ref[0].astype(BF16)), NEG_INF)
    s_n = jnp.where(mult_n > 0, _dot_nt(qe, kn_ref[0].astype(BF16)), NEG_INF)
    m = jnp.maximum(jnp.max(s_c, axis=-1, keepdims=True), jnp.max(s_n, axis=-1, keepdims=True))
    e_c = mult_c * jnp.exp(s_c - m)
    e_n = mult_n * jnp.exp(s_n - m)
    den = jnp.sum(e_c, axis=-1, keepdims=True) + jnp.sum(e_n, axis=-1, keepdims=True)
    o = (_dot_nt(e_c.astype(BF16), vc_ref[0].astype(BF16)) + _dot(e_n.astype(BF16), vn_ref[0].astype(BF16))) / den
    o = jnp.where(own, o, 0.0)
    acc = o[0:rows8]
    for h in range(1, A_HEADS):
        acc = acc + o[h * rows8:(h + 1) * rows8]
    o_ref[0] = acc


def attn_sample(z8, cache_k, cache_v, n_new):
    nb, rows8, _ = z8.shape
    win = cache_k.shape[2]
    return pl.pallas_call(
        functools.partial(_attn_sample_kernel, n_new=n_new),
        grid=(nb,),
        in_specs=[pl.BlockSpec((1, rows8, A_WIDTH), lambda b: (b, 0, 0)),
                  pl.BlockSpec((1, rows8, A_WIDTH), lambda b: (b, 0, 1)),
                  pl.BlockSpec((1, rows8, A_WIDTH), lambda b: (b, 0, 2)),
                  pl.BlockSpec((1, A_WIDTH, win), lambda b: (b, 0, 0)),
                  pl.BlockSpec((1, A_WIDTH, win), lambda b: (b, 0, 0))],
        out_specs=pl.BlockSpec((1, rows8, A_WIDTH), lambda b: (b, 0, 0)),
        out_shape=jax.ShapeDtypeStruct((nb, rows8, A_WIDTH), F32),
        compiler_params=_params("parallel"),
        name="attn_sample",
    )(z8, z8, z8, cache_k, cache_v)


def _hgrn_kernel(q_ref, f_ref, i_ref, lb_ref, sel_ref, s0_ref, o_ref, sn_ref, st_ref, *,
                 chunk, group, sub, n_valid, has_s0):
    c_idx = pl.program_id(1)

    @pl.when(c_idx == 0)
    def _():
        for h in range(B_HEADS):
            st_ref[h] = s0_ref[0, h].T if has_s0 else jnp.zeros((B_DV, B_DK), F32)

    lb = lb_ref[...]
    f = lb + (1.0 - lb) * jax.nn.sigmoid(f_ref[...])
    logf_all = jnp.log(f)
    kgate_all = 1.0 - f
    q_all = jax.nn.silu(q_ref[...])
    if n_valid < chunk:
        assert group == 1
        live = lax.broadcasted_iota(jnp.int32, (chunk, 1), 0) < n_valid
        logf_all = jnp.where(live, logf_all, 0.0)
        kgate_all = jnp.where(live, kgate_all, 0.0)
        q_all = jnp.where(live, q_all, 0.0)
    iv_all = i_ref[...]
    tr = lax.broadcasted_iota(jnp.int32, (chunk, chunk), 0)
    tc = lax.broadcasted_iota(jnp.int32, (chunk, chunk), 1)
    tril = (tr >= tc).astype(F32)

    nsub = chunk // sub
    srow = lax.broadcasted_iota(jnp.int32, (sub, B_DK), 0)
    krow = lax.broadcasted_iota(jnp.int32, (chunk, 1), 0)
    same_sub = (tr // sub) == (tc // sub)
    lane_sum = sel_ref[...]
    cat = (lambda xs: xs[0] if nsub == 1 else jnp.concatenate(xs, axis=0))
    states = [st_ref[h] for h in range(B_HEADS)]

    for g in range(group):
        rows = slice(g * chunk, (g + 1) * chunk)
        q, kgate, iv = q_all[rows], kgate_all[rows], iv_all[rows]
        cum = jnp.dot(tril, logf_all[rows], preferred_element_type=F32, precision=lax.Precision.HIGHEST)
        for h in range(B_HEADS):
            sl = slice(h * B_DK, (h + 1) * B_DK)
            c_h, q_h, k_h, i_h = cum[:, sl], q[:, sl], kgate[:, sl], iv[:, sl]
            st = states[h]
            i_b = i_h.astype(BF16)
            o_inter = _dot_nt((q_h * jnp.exp(c_h)).astype(BF16), st.astype(BF16))
            diag_rows, off_rows = [], []
            for blk in range(nsub):
                r0 = blk * sub
                c_i, q_i, k_i = c_h[r0:r0 + sub], q_h[r0:r0 + sub], k_h[r0:r0 + sub]
                parts = []
                for s in range(sub):
                    dec = jnp.exp(jnp.where(srow >= s, c_i - c_i[s:s + 1], NEG_INF))
                    parts.append((q_i * k_i[s:s + 1] * dec).astype(BF16))
                diag_rows.append(jnp.concatenate(parts, axis=1))
                if blk > 0:
                    c_ref = c_h[r0 - 1:r0]
                    q_t = (q_i * jnp.exp(c_i - c_ref)).astype(BF16)
                    k_t = jnp.where(krow < r0, k_h * jnp.exp(jnp.minimum(c_ref - c_h, 0.0)), 0.0).astype(BF16)
                    off_rows.append(_dot_nt(q_t, k_t))
                else:
                    off_rows.append(jnp.zeros((sub, chunk), F32))
            a_full = jnp.where(same_sub, _dot(cat(diag_rows), lane_sum), 0.0) + cat(off_rows)
            o_ref[rows, sl] = o_inter + _dot(a_full.astype(BF16), i_b)
            c_last = c_h[chunk - 1:chunk]
            k_dec = (k_h * jnp.exp(c_last - c_h)).astype(BF16)
            states[h] = st * jnp.exp(c_last) + _dot(i_h.T.astype(BF16), k_dec)

    for h in range(B_HEADS):
        st_ref[h] = states[h]

    @pl.when(c_idx == pl.num_programs(1) - 1)
    def _():
        for h in range(B_HEADS):
            sn_ref[0, h] = states[h].T


def hgrn(z, lb, s0, batch, seq, chunk, group, sub, n_valid):
    step = chunk * group
    nstep = seq // step
    qcol = 3 * A_WIDTH // B_WIDTH
    has_s0 = s0 is not None
    if s0 is None:
        s0 = jnp.zeros((1, B_HEADS, B_DK, B_DV), F32)
    s0_map = (lambda b, c: (b, 0, 0, 0)) if has_s0 else (lambda b, c: (0, 0, 0, 0))
    tok = lambda col: pl.BlockSpec((step, B_WIDTH), lambda b, c, col=col: (b * nstep + c, col))
    lane_sum = (np.arange(sub * B_DK)[:, None] // B_DK == np.arange(chunk)[None, :] % sub)
    lane_sum = jnp.asarray(lane_sum, dtype=BF16)
    return pl.pallas_call(
        functools.partial(_hgrn_kernel, chunk=chunk, group=group, sub=sub, n_valid=n_valid, has_s0=has_s0),
        grid=(batch, nstep),
        in_specs=[tok(qcol), tok(qcol + 1), tok(qcol + 2),
                  pl.BlockSpec((1, B_WIDTH), lambda b, c: (0, 0)),
                  pl.BlockSpec((sub * B_DK, chunk), lambda b, c: (0, 0)),
                  pl.BlockSpec((1, B_HEADS, B_DK, B_DV), s0_map)],
        out_specs=[pl.BlockSpec((step, B_WIDTH), lambda b, c: (b * nstep + c, 0)),
                   pl.BlockSpec((1, B_HEADS, B_DK, B_DV), lambda b, c: (b, 0, 0, 0))],
        out_shape=[jax.ShapeDtypeStruct((batch * seq, B_WIDTH), F32),
                   jax.ShapeDtypeStruct((batch, B_HEADS, B_DK, B_DV), F32)],
        scratch_shapes=[pltpu.VMEM((B_HEADS, B_DV, B_DK), F32)],
        compiler_params=_params("parallel", "arbitrary"),
        name="hgrn",
    )(z, z, z, lb, lane_sum, s0)


def _mix_out_kernel(x_ref, oa_ref, ob_ref, gb_ref, beta_ref, gn_ref, wout_ref, nc_ref, wcq_ref, x1_ref, qc_ref):
    parts = [_rms(oa_ref[...], beta_ref[...]).astype(BF16)]
    ob = ob_ref[...]
    gate = jax.nn.silu(gb_ref[...])
    gn = gn_ref[...]
    for h in range(B_HEADS):
        sl = slice(h * B_DV, (h + 1) * B_DV)
        parts.append((_rms(ob[:, sl], gn[:, sl]) * gate[:, sl]).astype(BF16))
    x1 = x_ref[...] + _dot(jnp.concatenate(parts, axis=-1), wout_ref[...])
    x1_ref[...] = x1
    qc_ref[...] = _dot(_rms(x1, nc_ref[...]).astype(BF16), wcq_ref[...])


def mix_out(x, oa, ob, z, beta, gn, wout, nc, wcq, tb):
    n = x.shape[0]
    gcol = MIX_IN // B_WIDTH - 1
    row = lambda w: pl.BlockSpec((tb, w), lambda i: (i, 0))
    const = lambda a: pl.BlockSpec(a.shape, lambda i: (0,) * a.ndim)
    args = (beta.reshape(1, A_WIDTH), gn.reshape(1, B_WIDTH), wout, nc.reshape(1, D_MODEL), wcq)
    return pl.pallas_call(
        _mix_out_kernel,
        grid=(n // tb,),
        in_specs=[row(D_MODEL), row(A_WIDTH), row(B_WIDTH), pl.BlockSpec((tb, B_WIDTH), lambda i: (i, gcol))]
        + [const(a) for a in args],
        out_specs=[row(D_MODEL), row(X_WIDTH)],
        out_shape=[jax.ShapeDtypeStruct((n, D_MODEL), F32), jax.ShapeDtypeStruct((n, X_WIDTH), F32)],
        compiler_params=_params("parallel"),
        name="mix_out",
    )(x, oa, ob, z, *args)


def _cross_kernel(q_ref, k_ref, v_ref, o_ref):
    scale = X_HD ** -0.5
    for h in range(X_HEADS):
        sl = slice(h * X_HD, (h + 1) * X_HD)
        s = _dot_nt(q_ref[:, sl].astype(BF16), k_ref[:, sl].astype(BF16)) * scale
        m = jnp.max(s, axis=-1, keepdims=True)
        p = jnp.exp(s - m)
        den = jnp.sum(p, axis=-1, keepdims=True)
        o_ref[:, sl] = _dot(p.astype(BF16), v_ref[:, sl].astype(BF16)) / den


def cross(qc, mk, mv, kcol, vcol, batch, tq):
    n = qc.shape[0]
    per = n // batch // tq
    return pl.pallas_call(
        _cross_kernel,
        grid=(batch, per),
        in_specs=[pl.BlockSpec((tq, X_WIDTH), lambda b, i: (b * per + i, 0)),
                  pl.BlockSpec((N_MEM, X_WIDTH), lambda b, i: (b, kcol)),
                  pl.BlockSpec((N_MEM, X_WIDTH), lambda b, i: (b, vcol))],
        out_specs=pl.BlockSpec((tq, X_WIDTH), lambda b, i: (b * per + i, 0)),
        out_shape=jax.ShapeDtypeStruct((n, X_WIDTH), F32),
        compiler_params=_params("parallel", "parallel"),
        name="cross",
    )(qc, mk, mv)


def _top_values(s, k):
    vals = []
    for _ in range(k):
        m = jnp.max(s, axis=0, keepdims=True)
        vals.append(m)
        s = jnp.where(s == m, NEG_INF, s)
    return jnp.concatenate(vals, axis=0)


def _merge_sort_pairs(n):
    pairs = []
    p = 1
    while p < n:
        k = p
        while k >= 1:
            for j in range(k % p, n - k, 2 * k):
                for i in range(min(k, n - j - k)):
                    if (i + j) // (2 * p) == (i + j + k) // (2 * p):
                        pairs.append((i + j, i + j + k))
            k //= 2
        p *= 2
    return pairs


def _sort_desc(xs, pairs):
    xs = list(xs)
    for i, j in pairs:
        xs[i], xs[j] = jnp.maximum(xs[i], xs[j]), jnp.minimum(xs[i], xs[j])
    return xs


def _top16_sorted(s):
    n = PEER_TOPK
    sub = s.shape[0] // n
    xs = _sort_desc([s[r * sub:(r + 1) * sub] for r in range(n)], _merge_sort_pairs(n))
    bitonic = [(i, i + d) for d in (8, 4, 2, 1) for i in range(n) if (i // d) % 2 == 0]
    for shift in (4, 2, 1):
        ys = [pltpu.roll(x, shift, axis=0) for x in xs]
        xs = _sort_desc([jnp.maximum(xs[i], ys[n - 1 - i]) for i in range(n)], bitonic)
    return xs


def _rows(slabs):
    row = lax.broadcasted_iota(jnp.int32, slabs[0].shape, 0)
    out = slabs[-1]
    for i in range(len(slabs) - 2, -1, -1):
        out = jnp.where(row == i, slabs[i], out)
    return out


def _pair_sums(x1, v1_hi, x2_0, v2_lo, v2_hi):
    half = PEER_TOPK // 2
    row = lax.broadcasted_iota(jnp.int32, (half, 1), 0)
    parts = [x1[0] + v2_lo, x1[0] + v2_hi]
    for a in range(1, half):
        parts.append(jnp.where(row < PEER_TOPK // (a + 1), x1[a] + v2_lo, NEG_INF))
    parts.append(v1_hi + x2_0)
    return jnp.concatenate(parts, axis=0)


def _peer_pre_kernel(x1_ref, oc_ref, wco_ref, nf_ref, wpq_ref, k1_ref, k2_ref,
                     x2_ref, ht_ref, s1_ref, s2_ref, tau_ref):
    x2 = x1_ref[...] + _dot(oc_ref[...].astype(BF16), wco_ref[...])
    x2_ref[...] = x2
    ht = _rms(x2, nf_ref[...]).T.astype(BF16)
    ht_ref[...] = ht
    qt = _dot(wpq_ref[...], ht)
    log2e = float(np.log2(np.e))

    def scores(keys3, q):
        q_hi = q.astype(BF16)
        q_lo = (q - q_hi.astype(F32)).astype(BF16)
        return _dot(keys3, jnp.concatenate([q_hi, q_lo, q_hi], axis=0)) * log2e

    for h in range(PEER_HEADS):
        r0 = h * 2 * PEER_HALF
        s1 = scores(k1_ref[h], qt[r0:r0 + PEER_HALF])
        s2 = scores(k2_ref[h], qt[r0 + PEER_HALF:r0 + 2 * PEER_HALF])
        x1 = _top16_sorted(s1)
        x2 = _top16_sorted(s2)
        half = PEER_TOPK // 2
        v1_hi, v2_lo, v2_hi = _rows(x1[half:]), _rows(x2[:half]), _rows(x2[half:])
        cand = _pair_sums(x1, v1_hi, x2[0], v2_lo, v2_hi)
        sc = _top_values(cand, PEER_TOPK)
        top = sc[0:1]
        shift = top + jnp.log2(jnp.sum(jnp.exp2(sc - top), axis=0, keepdims=True))
        picked = cand >= sc[PEER_TOPK - 1:PEER_TOPK]
        shifted = _pair_sums(x1, v1_hi, x2[0] - shift, v2_lo - shift, v2_hi - shift)
        tau = jnp.min(jnp.where(picked, shifted, jnp.inf), axis=0, keepdims=True)
        s1_ref[h] = s1
        s2_ref[h] = s2 - shift
        tau_ref[h:h + 1, :] = tau


def peer_pre(x1, oc, wco, nf, wpq_t, k1, k2, tb):
    n = x1.shape[0]
    const = lambda a: pl.BlockSpec(a.shape, lambda i: (0,) * a.ndim)
    args = (wco, nf.reshape(1, D_MODEL), wpq_t, k1, k2)
    return pl.pallas_call(
        _peer_pre_kernel,
        grid=(n // tb,),
        in_specs=[pl.BlockSpec((tb, D_MODEL), lambda i: (i, 0)), pl.BlockSpec((tb, X_WIDTH), lambda i: (i, 0))]
        + [const(a) for a in args],
        out_specs=[pl.BlockSpec((tb, D_MODEL), lambda i: (i, 0)),
                   pl.BlockSpec((D_MODEL, tb), lambda i: (0, i)),
                   pl.BlockSpec((PEER_HEADS, PEER_NKEYS, tb), lambda i: (0, 0, i)),
                   pl.BlockSpec((PEER_HEADS, PEER_NKEYS, tb), lambda i: (0, 0, i)),
                   pl.BlockSpec((PEER_HEADS, tb), lambda i: (0, i))],
        out_shape=[jax.ShapeDtypeStruct((n, D_MODEL), F32),
                   jax.ShapeDtypeStruct((D_MODEL, n), BF16),
                   jax.ShapeDtypeStruct((PEER_HEADS, PEER_NKEYS, n), F32),
                   jax.ShapeDtypeStruct((PEER_HEADS, PEER_NKEYS, n), F32),
                   jax.ShapeDtypeStruct((PEER_HEADS, n), F32)],
        compiler_params=_params("parallel"),
        name="peer_pre",
    )(x1, oc, *args)


PEER_ROWS = 16
PEER_EBLOCK = 1024
PEER_ECHUNK = 512
PEER_TCHUNK = 256


def _peer_dense_kernel(ht_ref, s1_ref, s2_ref, tau_ref, u_ref, vt_ref, x2_ref, g_ref,
                       y_ref, acc_ref, *pieces, eb, tb):
    j = pl.program_id(1)

    @pl.when(j == 0)
    def _():
        acc_ref[...] = jnp.zeros_like(acc_ref)

    ec, tcw = PEER_ECHUNK, min(PEER_TCHUNK, tb)
    n_ec, n_tc = eb // ec, tb // tcw
    at_refs = [pieces[me * n_tc:(me + 1) * n_tc] for me in range(n_ec)]
    ct_refs = pieces[n_ec * n_tc:]

    def pre_act(me, nt):
        at_refs[me][nt][...] = _dot(u_ref[me * ec:(me + 1) * ec, :], ht_ref[:, nt * tcw:(nt + 1) * tcw])

    def strip(me, nt):
        n_loc = ec // PEER_NKEYS
        for tc in range(tcw // LANES):
            ls = slice(tc * LANES, (tc + 1) * LANES)
            gs = slice(nt * tcw + tc * LANES, nt * tcw + (tc + 1) * LANES)
            for rb in range(PEER_NKEYS // PEER_ROWS):
                rs = slice(rb * PEER_ROWS, (rb + 1) * PEER_ROWS)
                w = [jnp.zeros((PEER_ROWS, LANES), F32) for _ in range(n_loc)]
                for h in range(PEER_HEADS):
                    s2 = s2_ref[h, rs, gs]
                    tau = tau_ref[h:h + 1, gs]
                    for il in range(n_loc):
                        ii = me * n_loc + il
                        sm = s2 + s1_ref[h, ii:ii + 1, gs]
                        w[il] = w[il] + jnp.where(sm >= tau, jnp.exp2(sm), 0.0)
                for il in range(n_loc):
                    r0 = il * PEER_NKEYS + rb * PEER_ROWS
                    a = at_refs[me][nt][r0:r0 + PEER_ROWS, ls]
                    gelu = 0.5 * a * (1.0 + lax.erf(a * (2.0 ** -0.5)))
                    ct_refs[nt][me * ec + r0:me * ec + r0 + PEER_ROWS, ls] = (w[il] * gelu).astype(BF16)

    def project(mo, nt):
        rows = slice(mo * (D_MODEL // n_ec), (mo + 1) * (D_MODEL // n_ec))
        acc_ref[rows, nt * tcw:(nt + 1) * tcw] += _dot(vt_ref[0, rows, :], ct_refs[nt][...])

    order = [(me, nt) for nt in range(n_tc) for me in range(n_ec)]
    pre_act(*order[0])
    for k, (me, nt) in enumerate(order):
        if k + 1 < len(order):
            pre_act(*order[k + 1])
        if nt > 0:
            project(me, nt - 1)
        strip(me, nt)
    for mo in range(n_ec):
        project(mo, n_tc - 1)

    @pl.when(j == pl.num_programs(1) - 1)
    def _():
        y_ref[...] = _rms(x2_ref[...] + acc_ref[...].T, g_ref[...])


def peer_dense(ht, s1, s2, tau, u, vt, x2, g, tb, eb):
    n = x2.shape[0]
    tok = lambda rows: pl.BlockSpec((rows, tb), lambda i, j: (0, i))
    tcw = min(PEER_TCHUNK, tb)
    pieces = ([pltpu.VMEM((PEER_ECHUNK, tcw), F32)] * ((eb // PEER_ECHUNK) * (tb // tcw))
              + [pltpu.VMEM((eb, tcw), BF16)] * (tb // tcw))
    return pl.pallas_call(
        functools.partial(_peer_dense_kernel, eb=eb, tb=tb),
        grid=(n // tb, PEER_N // eb),
        in_specs=[tok(D_MODEL), pl.BlockSpec((PEER_HEADS, eb // PEER_NKEYS, tb), lambda i, j: (0, j, i)),
                  pl.BlockSpec((PEER_HEADS, PEER_NKEYS, tb), lambda i, j: (0, 0, i)), tok(PEER_HEADS),
                  pl.BlockSpec((eb, D_MODEL), lambda i, j: (j, 0)),
                  pl.BlockSpec((1, D_MODEL, eb), lambda i, j: (j, 0, 0)),
                  pl.BlockSpec((tb, D_MODEL), lambda i, j: (i, 0)),
                  pl.BlockSpec((1, D_MODEL), lambda i, j: (0, 0))],
        out_specs=pl.BlockSpec((tb, D_MODEL), lambda i, j: (i, 0)),
        out_shape=jax.ShapeDtypeStruct((n, D_MODEL), F32),
        scratch_shapes=[pltpu.VMEM((D_MODEL, tb), F32)] + pieces,
        compiler_params=_params("parallel", "arbitrary"),
        name="peer_dense",
    )(ht, s1, s2, tau, u, vt, x2, g.reshape(1, D_MODEL))


def _split3(k):
    hi = k.astype(BF16)
    lo = (k - hi.astype(F32)).astype(BF16)
    return jnp.concatenate([hi, hi, lo], axis=-1)


def _trunk(x, z, oa, ob, mk, mv, kcol, vcol, batch, tq, w, tb_tok, tb_peer, pad_rows):
    x1, qc = mix_out(x, oa, ob, z, w["beta_a"], w["gnorm_b"], w["w_out"], w["norm_cross"], w["w_cq"], tb_tok)
    if pad_rows:
        n_new = x.shape[0] // batch
        qc = jnp.pad(qc.reshape(batch, n_new, X_WIDTH), ((0, 0), (0, pad_rows - n_new), (0, 0)))
        oc = cross(qc.reshape(batch * pad_rows, X_WIDTH), mk, mv, kcol, vcol, batch, tq)
        oc = oc.reshape(batch, pad_rows, X_WIDTH)[:, :n_new].reshape(x.shape[0], X_WIDTH)
    else:
        oc = cross(qc, mk, mv, kcol, vcol, batch, tq)
    x2, ht, s1, s2, tau = peer_pre(x1, oc, w["w_co"], w["norm_ffn"], w["w_pq_t"], w["k1"], w["k2"], tb_tok)
    return peer_dense(ht, s1, s2, tau, w["u"], w["v_blocks"], x2, w["norm_final"], tb_peer, PEER_EBLOCK)


def kernel(x_prompt, x_sample, cache_swa_k, cache_swa_v, state_hgrn, cache_mem_k, cache_mem_v, mem_prompt, norm_mix, w_in, lb_logits, beta_a, gnorm_b, w_out, norm_cross, norm_mem, w_cq, w_mk, w_mv, w_co, norm_ffn, w_pq, peer_k1, peer_k2, peer_u, peer_v, norm_final):
    batch, seq, _ = x_prompt.shape
    dbatch, dseq, _ = x_sample.shape
    win = cache_swa_k.shape[2]
    rows8 = 8
    layer = 0
    lb = jnp.cumsum(jax.nn.softmax(lb_logits.astype(F32), axis=0), axis=0)[layer].reshape(1, B_WIDTH)
    w = {
        "beta_a": beta_a[layer], "gnorm_b": gnorm_b[layer], "w_out": w_out[layer].astype(BF16),
        "norm_cross": norm_cross[layer], "w_cq": w_cq[layer].astype(BF16), "w_co": w_co[layer].astype(BF16),
        "norm_ffn": norm_ffn[layer], "w_pq_t": w_pq[layer].T.astype(BF16),
        "k1": _split3(peer_k1[layer]), "k2": _split3(peer_k2[layer]),
        "u": peer_u[layer].astype(BF16), "norm_final": norm_final,
        "v_blocks": peer_v[layer].astype(BF16).reshape(PEER_N // PEER_EBLOCK, PEER_EBLOCK, D_MODEL).transpose(0, 2, 1),
    }
    w_in_b = w_in[layer].astype(BF16)
    w_mem = jnp.concatenate([w_mk[layer], w_mv[layer]], axis=1).astype(BF16)

    xp = x_prompt.reshape(batch * seq, D_MODEL)
    zp = norm_matmul(xp, norm_mix[layer], w_in_b, 256)
    oa_p = attn_prompt(zp, batch, seq)
    ob_p, st_p = hgrn(zp, lb, None, batch, seq, 64, HGRN_GROUP, 16, 64)
    memkv = norm_matmul(mem_prompt.reshape(batch * N_MEM, D_MODEL), norm_mem[layer], w_mem, 256)
    y_p = _trunk(xp, zp, oa_p, ob_p, memkv, memkv, 0, 1, batch, 512, w, 256, 1024, 0)

    xs = x_sample.reshape(dbatch * dseq, D_MODEL)
    zs = norm_matmul(xs, norm_mix[layer], w_in_b, dbatch * dseq)
    zs8 = jnp.pad(zs.reshape(dbatch, dseq, MIX_IN), ((0, 0), (0, rows8 - dseq), (0, 0)))
    feature_major = lambda c: jnp.transpose(c, (0, 2, 3, 1)).reshape(dbatch, A_WIDTH, win)
    oa_s = attn_sample(zs8, feature_major(cache_swa_k[layer]), feature_major(cache_swa_v[layer]), dseq)
    oa_s = oa_s[:, :dseq].reshape(dbatch * dseq, A_WIDTH)
    ob_s, st_s = hgrn(zs8.reshape(dbatch * rows8, MIX_IN), lb, state_hgrn[layer], dbatch, rows8, rows8, 1, rows8, dseq)
    ob_s = ob_s.reshape(dbatch, rows8, B_WIDTH)[:, :dseq].reshape(dbatch * dseq, B_WIDTH)
    y_s = _trunk(xs, zs, oa_s, ob_s, cache_mem_k[layer].reshape(dbatch * N_MEM, X_WIDTH),
                 cache_mem_v[layer].reshape(dbatch * N_MEM, X_WIDTH), 0, 0, dbatch, rows8, w,
                 dbatch * dseq, dbatch * dseq, rows8)

    keep = min(BRANCHES[-1][0], seq)
    kv_t = window_rows(zp, batch, seq, keep)
    kp, vp = (jnp.transpose(kv_t[i].reshape(batch, A_HEADS, A_HD, keep), (0, 3, 1, 2))[None] for i in range(2))
    return (y_p.reshape(batch, seq, D_MODEL), y_s.reshape(dbatch, dseq, D_MODEL),
            kp, vp, st_p[None],
            memkv[:, :X_WIDTH].reshape(1, batch, N_MEM, X_HEADS, X_HD),
            memkv[:, X_WIDTH:].reshape(1, batch, N_MEM, X_HEADS, X_HD),
            zs[:, A_WIDTH:2 * A_WIDTH].reshape(1, dbatch, dseq, A_HEADS, A_HD),
            zs[:, 2 * A_WIDTH:3 * A_WIDTH].reshape(1, dbatch, dseq, A_HEADS, A_HD),
            st_s[None])
```

```python
import functools

import jax
import jax.numpy as jnp
import numpy as np
from jax import lax
from jax.experimental import pallas as pl
from jax.experimental.pallas import tpu as pltpu

F32 = jnp.float32
BF16 = jnp.bfloat16
EPS = 1e-6
NEG_INF = float("-inf")

LANES = 128
D_MODEL = 1024
A_HEADS, A_HD = 8, 64
A_WIDTH = A_HEADS * A_HD
BRANCHES = ((128, 1), (512, 4), (2048, 16))
SPAN = 128
ATTN_UNROLL = 8
HGRN_GROUP = 8
B_HEADS, B_DK, B_DV = 4, 128, 128
B_WIDTH = B_HEADS * B_DV
MIX_IN = 3 * A_WIDTH + 2 * B_HEADS * B_DK + 2 * B_WIDTH
N_MEM = 256
X_HEADS, X_HD = 4, 128
X_WIDTH = X_HEADS * X_HD
PEER_HEADS = 8
PEER_NKEYS = 128
PEER_N = PEER_NKEYS * PEER_NKEYS
PEER_HALF = 128
PEER_TOPK = 16
VMEM_LIMIT = 56 * 1024 * 1024


def _params(*sem):
    return pltpu.CompilerParams(dimension_semantics=sem, vmem_limit_bytes=VMEM_LIMIT)


def _rms(x, g):
    return x * lax.rsqrt(jnp.mean(x * x, axis=-1, keepdims=True) + EPS) * g


def _dot(a, b):
    return jnp.dot(a, b, preferred_element_type=F32)


def _dot_nt(a, b):
    return lax.dot_general(a, b, (((1,), (1,)), ((), ())), preferred_element_type=F32)


def _norm_matmul_kernel(x_ref, g_ref, w_ref, o_ref):
    h = _rms(x_ref[...], g_ref[...])
    o_ref[...] = _dot(h.astype(BF16), w_ref[...])


def norm_matmul(x, g, w, tb):
    n, d = x.shape
    m = w.shape[1]
    return pl.pallas_call(
        _norm_matmul_kernel,
        grid=(n // tb,),
        in_specs=[pl.BlockSpec((tb, d), lambda i: (i, 0)),
                  pl.BlockSpec((1, d), lambda i: (0, 0)),
                  pl.BlockSpec((d, m), lambda i: (0, 0))],
        out_specs=pl.BlockSpec((tb, m), lambda i: (i, 0)),
        out_shape=jax.ShapeDtypeStruct((n, m), F32),
        compiler_params=_params("parallel"),
        name="norm_matmul",
    )(x, g.reshape(1, d), w)


def _window_rows_kernel(x_ref, o_ref):
    o_ref[0, 0] = x_ref[...].T


def window_rows(z, batch, seq, keep):
    npair = A_WIDTH // LANES
    last = seq // keep - 1
    return pl.pallas_call(
        _window_rows_kernel,
        grid=(2, batch, npair),
        in_specs=[pl.BlockSpec((keep, LANES), lambda kv, b, p: (b * (seq // keep) + last, (1 + kv) * npair + p))],
        out_specs=pl.BlockSpec((1, 1, LANES, keep), lambda kv, b, p: (kv, b, p, 0)),
        out_shape=jax.ShapeDtypeStruct((2, batch, A_WIDTH, keep), F32),
        compiler_params=_params("parallel", "parallel", "parallel"),
        name="window_rows",
    )(z)


def _attn_prompt_kernel(q_ref, k_ref, v_ref, o_ref, l_ref):
    seq = q_ref.shape[0]
    lane = lax.broadcasted_iota(jnp.int32, (1, LANES), 1)
    head0 = lane < A_HD
    qi = lax.broadcasted_iota(jnp.int32, (SPAN, 2 * SPAN), 0)
    kk = lax.broadcasted_iota(jnp.int32, (SPAN, 2 * SPAN), 1)
    dist = SPAN + qi - kk
    band = (dist >= 0) & (dist <= SPAN)
    scale = A_HD ** -0.5

    for bi, (window, dil) in enumerate(BRANCHES):
        assert window // dil == SPAN
        step = SPAN * dil

        def rows(start, dil=dil):
            return pl.ds(start, SPAN, stride=dil) if dil > 1 else pl.ds(start, SPAN)

        def block(idx, bi=bi, dil=dil, step=step, rows=rows):
            r = idx % dil
            blk = idx // dil
            cur = r + blk * step
            prev = r + jnp.maximum(blk - 1, 0) * step
            q = q_ref[rows(cur), :] * scale
            kc = jnp.concatenate([k_ref[rows(prev), :], k_ref[rows(cur), :]], axis=0).astype(BF16)
            vc = jnp.concatenate([v_ref[rows(prev), :], v_ref[rows(cur), :]], axis=0).astype(BF16)
            old = (o_ref[rows(cur), :], l_ref[rows(cur), :]) if bi > 0 else None
            valid = band & ((kk >= SPAN) | (blk > 0))
            outs, lses = [], []
            for hh in range(2):
                hm = head0 if hh == 0 else jnp.logical_not(head0)
                qh = jnp.where(hm, q, 0.0).astype(BF16)
                s = jnp.where(valid, _dot_nt(qh, kc), NEG_INF)
                m = jnp.max(s, axis=-1, keepdims=True)
                p = jnp.exp(s - m)
                den = jnp.sum(p, axis=-1, keepdims=True)
                outs.append(_dot(p.astype(BF16), vc) * (1.0 / den))
                lses.append(m + jnp.log(den))
            o_new = jnp.where(head0, outs[0], outs[1])
            l_new = jnp.where(head0, lses[0], lses[1])
            if old is not None:
                o_old, l_old = old
                mx = jnp.maximum(l_old, l_new)
                wa = jnp.exp(l_old - mx)
                wb = jnp.exp(l_new - mx)
                tot = wa + wb
                o_new = (wa * o_old + wb * o_new) * (1.0 / tot)
                l_new = mx + jnp.log(tot)
            return cur, o_new, l_new

        def body(g, carry, rows=rows, block=block):
            done = [block(g * ATTN_UNROLL + u) for u in range(ATTN_UNROLL)]
            for cur, o_new, l_new in done:
                o_ref[rows(cur), :] = o_new
                l_ref[rows(cur), :] = l_new
            return carry

        lax.fori_loop(0, seq // SPAN // ATTN_UNROLL, body, 0)


def attn_prompt(z, batch, seq):
    npair = A_WIDTH // LANES
    return pl.pallas_call(
        _attn_prompt_kernel,
        grid=(batch, npair),
        in_specs=[pl.BlockSpec((seq, LANES), lambda b, p: (b, p)),
                  pl.BlockSpec((seq, LANES), lambda b, p: (b, npair + p)),
                  pl.BlockSpec((seq, LANES), lambda b, p: (b, 2 * npair + p))],
        out_specs=pl.BlockSpec((seq, LANES), lambda b, p: (b, p)),
        out_shape=jax.ShapeDtypeStruct((batch * seq, A_WIDTH), F32),
        scratch_shapes=[pltpu.VMEM((seq, LANES), F32)],
        compiler_params=_params("parallel", "parallel"),
        name="attn_prompt",
    )(z, z, z)


def _attn_sample_kernel(q_ref, kn_ref, vn_ref, kc_ref, vc_ref, o_ref, *, n_new):
    win = kc_ref.shape[2]
    rows8 = q_ref.shape[1]
    nrow = A_HEADS * rows8
    q8 = q_ref[0] * (A_HD ** -0.5)
    qe = jnp.concatenate([q8] * A_HEADS, axis=0)
    row = lax.broadcasted_iota(jnp.int32, (nrow, A_WIDTH), 0)
    col = lax.broadcasted_iota(jnp.int32, (nrow, A_WIDTH), 1)
    own = (col // A_HD) == (row // rows8)
    qe = jnp.where(own, qe, 0.0).astype(BF16)

    def multiplicity(delta, ok):
        mult = jnp.zeros(delta.shape, F32)
        for window, dil in BRANCHES:
            hit = ok & (delta >= 0) & ((delta & (dil - 1)) == 0) & (delta <= window)
            mult = mult + hit.astype(F32)
        return mult

    t_c = lax.broadcasted_iota(jnp.int32, (nrow, win), 0) % rows8
    p_c = lax.broadcasted_iota(jnp.int32, (nrow, win), 1)
    mult_c = multiplicity(win + t_c - p_c, p_c >= 0)
    t_n = lax.broadcasted_iota(jnp.int32, (nrow, rows8), 0) % rows8
    u_n = lax.broadcasted_iota(jnp.int32, (nrow, rows8), 1)
    mult_n = multiplicity(t_n - u_n, u_n < n_new)

    s_c = jnp.where(mult_c > 0, _dot(qe, kc_ref[0].astype(BF16)), NEG_INF)
    s_n = jnp.where(mult_n > 0, _dot_nt(qe, kn_ref[0].astype(BF16)), NEG_INF)
    m = jnp.maximum(jnp.max(s_c, axis=-1, keepdims=True), jnp.max(s_n, axis=-1, keepdims=True))
    e_c = mult_c * jnp.exp(s_c - m)
    e_n = mult_n * jnp.exp(s_n - m)
    den = jnp.sum(e_c, axis=-1, keepdims=True) + jnp.sum(e_n, axis=-1, keepdims=True)
    o = (_dot_nt(e_c.astype(BF16), vc_ref[0].astype(BF16)) + _dot(e_n.astype(BF16), vn_ref[0].astype(BF16))) / den
    o = jnp.where(own, o, 0.0)
    acc = o[0:rows8]
    for h in range(1, A_HEADS):
        acc = acc + o[h * rows8:(h + 1) * rows8]
    o_ref[0] = acc


def attn_sample(z8, cache_k, cache_v, n_new):
    nb, rows8, _ = z8.shape
    win = cache_k.shape[2]
    return pl.pallas_call(
        functools.partial(_attn_sample_kernel, n_new=n_new),
        grid=(nb,),
        in_specs=[pl.BlockSpec((1, rows8, A_WIDTH), lambda b: (b, 0, 0)),
                  pl.BlockSpec((1, rows8, A_WIDTH), lambda b: (b, 0, 1)),
                  pl.BlockSpec((1, rows8, A_WIDTH), lambda b: (b, 0, 2)),
                  pl.BlockSpec((1, A_WIDTH, win), lambda b: (b, 0, 0)),
                  pl.BlockSpec((1, A_WIDTH, win), lambda b: (b, 0, 0))],
        out_specs=pl.BlockSpec((1, rows8, A_WIDTH), lambda b: (b, 0, 0)),
        out_shape=jax.ShapeDtypeStruct((nb, rows8, A_WIDTH), F32),
        compiler_params=_params("parallel"),
        name="attn_sample",
    )(z8, z8, z8, cache_k, cache_v)


def _hgrn_kernel(q_ref, f_ref, i_ref, lb_ref, sel_ref, s0_ref, o_ref, sn_ref, st_ref, *,
                 chunk, group, sub, n_valid, has_s0):
    c_idx = pl.program_id(1)

    @pl.when(c_idx == 0)
    def _():
        for h in range(B_HEADS):
            st_ref[h] = s0_ref[0, h].T if has_s0 else jnp.zeros((B_DV, B_DK), F32)

    lb = lb_ref[...]
    f = lb + (1.0 - lb) * jax.nn.sigmoid(f_ref[...])
    logf_all = jnp.log(f)
    kgate_all = 1.0 - f
    q_all = jax.nn.silu(q_ref[...])
    if n_valid < chunk:
        assert group == 1
        live = lax.broadcasted_iota(jnp.int32, (chunk, 1), 0) < n_valid
        logf_all = jnp.where(live, logf_all, 0.0)
        kgate_all = jnp.where(live, kgate_all, 0.0)
        q_all = jnp.where(live, q_all, 0.0)
    iv_all = i_ref[...]
    tr = lax.broadcasted_iota(jnp.int32, (chunk, chunk), 0)
    tc = lax.broadcasted_iota(jnp.int32, (chunk, chunk), 1)
    tril = (tr >= tc).astype(F32)

    nsub = chunk // sub
    srow = lax.broadcasted_iota(jnp.int32, (sub, B_DK), 0)
    krow = lax.broadcasted_iota(jnp.int32, (chunk, 1), 0)
    same_sub = (tr // sub) == (tc // sub)
    lane_sum = sel_ref[...]
    cat = (lambda xs: xs[0] if nsub == 1 else jnp.concatenate(xs, axis=0))
    states = [st_ref[h] for h in range(B_HEADS)]

    for g in range(group):
        rows = slice(g * chunk, (g + 1) * chunk)
        q, kgate, iv = q_all[rows], kgate_all[rows], iv_all[rows]
        cum = jnp.dot(tril, logf_all[rows], preferred_element_type=F32, precision=lax.Precision.HIGHEST)
        for h in range(B_HEADS):
            sl = slice(h * B_DK, (h + 1) * B_DK)
            c_h, q_h, k_h, i_h = cum[:, sl], q[:, sl], kgate[:, sl], iv[:, sl]
            st = states[h]
            i_b = i_h.astype(BF16)
            o_inter = _dot_nt((q_h * jnp.exp(c_h)).astype(BF16), st.astype(BF16))
            diag_rows, off_rows = [], []
            for blk in range(nsub):
                r0 = blk * sub
                c_i, q_i, k_i = c_h[r0:r0 + sub], q_h[r0:r0 + sub], k_h[r0:r0 + sub]
                parts = []
                for s in range(sub):
                    dec = jnp.exp(jnp.where(srow >= s, c_i - c_i[s:s + 1], NEG_INF))
                    parts.append((q_i * k_i[s:s + 1] * dec).astype(BF16))
                diag_rows.append(jnp.concatenate(parts, axis=1))
                if blk > 0:
                    c_ref = c_h[r0 - 1:r0]
                    q_t = (q_i * jnp.exp(c_i - c_ref)).astype(BF16)
                    k_t = jnp.where(krow < r0, k_h * jnp.exp(jnp.minimum(c_ref - c_h, 0.0)), 0.0).astype(BF16)
                    off_rows.append(_dot_nt(q_t, k_t))
                else:
                    off_rows.append(jnp.zeros((sub, chunk), F32))
            a_full = jnp.where(same_sub, _dot(cat(diag_rows), lane_sum), 0.0) + cat(off_rows)
            o_ref[rows, sl] = o_inter + _dot(a_full.astype(BF16), i_b)
            c_last = c_h[chunk - 1:chunk]
            k_dec = (k_h * jnp.exp(c_last - c_h)).astype(BF16)
            states[h] = st * jnp.exp(c_last) + _dot(i_h.T.astype(BF16), k_dec)

    for h in range(B_HEADS):
        st_ref[h] = states[h]

    @pl.when(c_idx == pl.num_programs(1) - 1)
    def _():
        for h in range(B_HEADS):
            sn_ref[0, h] = states[h].T


def hgrn(z, lb, s0, batch, seq, chunk, group, sub, n_valid):
    step = chunk * group
    nstep = seq // step
    qcol = 3 * A_WIDTH // B_WIDTH
    has_s0 = s0 is not None
    if s0 is None:
        s0 = jnp.zeros((1, B_HEADS, B_DK, B_DV), F32)
    s0_map = (lambda b, c: (b, 0, 0, 0)) if has_s0 else (lambda b, c: (0, 0, 0, 0))
    tok = lambda col: pl.BlockSpec((step, B_WIDTH), lambda b, c, col=col: (b * nstep + c, col))
    lane_sum = (np.arange(sub * B_DK)[:, None] // B_DK == np.arange(chunk)[None, :] % sub)
    lane_sum = jnp.asarray(lane_sum, dtype=BF16)
    return pl.pallas_call(
        functools.partial(_hgrn_kernel, chunk=chunk, group=group, sub=sub, n_valid=n_valid, has_s0=has_s0),
        grid=(batch, nstep),
        in_specs=[tok(qcol), tok(qcol + 1), tok(qcol + 2),
                  pl.BlockSpec((1, B_WIDTH), lambda b, c: (0, 0)),
                  pl.BlockSpec((sub * B_DK, chunk), lambda b, c: (0, 0)),
                  pl.BlockSpec((1, B_HEADS, B_DK, B_DV), s0_map)],
        out_specs=[pl.BlockSpec((step, B_WIDTH), lambda b, c: (b * nstep + c, 0)),
                   pl.BlockSpec((1, B_HEADS, B_DK, B_DV), lambda b, c: (b, 0, 0, 0))],
        out_shape=[jax.ShapeDtypeStruct((batch * seq, B_WIDTH), F32),
                   jax.ShapeDtypeStruct((batch, B_HEADS, B_DK, B_DV), F32)],
        scratch_shapes=[pltpu.VMEM((B_HEADS, B_DV, B_DK), F32)],
        compiler_params=_params("parallel", "arbitrary"),
        name="hgrn",
    )(z, z, z, lb, lane_sum, s0)


def _mix_out_kernel(x_ref, oa_ref, ob_ref, gb_ref, beta_ref, gn_ref, wout_ref, nc_ref, wcq_ref, x1_ref, qc_ref):
    parts = [_rms(oa_ref[...], beta_ref[...]).astype(BF16)]
    ob = ob_ref[...]
    gate = jax.nn.silu(gb_ref[...])
    gn = gn_ref[...]
    for h in range(B_HEADS):
        sl = slice(h * B_DV, (h + 1) * B_DV)
        parts.append((_rms(ob[:, sl], gn[:, sl]) * gate[:, sl]).astype(BF16))
    x1 = x_ref[...] + _dot(jnp.concatenate(parts, axis=-1), wout_ref[...])
    x1_ref[...] = x1
    qc_ref[...] = _dot(_rms(x1, nc_ref[...]).astype(BF16), wcq_ref[...])


def mix_out(x, oa, ob, z, beta, gn, wout, nc, wcq, tb):
    n = x.shape[0]
    gcol = MIX_IN // B_WIDTH - 1
    row = lambda w: pl.BlockSpec((tb, w), lambda i: (i, 0))
    const = lambda a: pl.BlockSpec(a.shape, lambda i: (0,) * a.ndim)
    args = (beta.reshape(1, A_WIDTH), gn.reshape(1, B_WIDTH), wout, nc.reshape(1, D_MODEL), wcq)
    return pl.pallas_call(
        _mix_out_kernel,
        grid=(n // tb,),
        in_specs=[row(D_MODEL), row(A_WIDTH), row(B_WIDTH), pl.BlockSpec((tb, B_WIDTH), lambda i: (i, gcol))]
        + [const(a) for a in args],
        out_specs=[row(D_MODEL), row(X_WIDTH)],
        out_shape=[jax.ShapeDtypeStruct((n, D_MODEL), F32), jax.ShapeDtypeStruct((n, X_WIDTH), F32)],
        compiler_params=_params("parallel"),
        name="mix_out",
    )(x, oa, ob, z, *args)


def _cross_kernel(q_ref, k_ref, v_ref, o_ref):
    scale = X_HD ** -0.5
    for h in range(X_HEADS):
        sl = slice(h * X_HD, (h + 1) * X_HD)
        s = _dot_nt(q_ref[:, sl].astype(BF16), k_ref[:, sl].astype(BF16)) * scale
        m = jnp.max(s, axis=-1, keepdims=True)
        p = jnp.exp(s - m)
        den = jnp.sum(p, axis=-1, keepdims=True)
        o_ref[:, sl] = _dot(p.astype(BF16), v_ref[:, sl].astype(BF16)) / den


def cross(qc, mk, mv, kcol, vcol, batch, tq):
    n = qc.shape[0]
    per = n // batch // tq
    return pl.pallas_call(
        _cross_kernel,
        grid=(batch, per),
        in_specs=[pl.BlockSpec((tq, X_WIDTH), lambda b, i: (b * per + i, 0)),
                  pl.BlockSpec((N_MEM, X_WIDTH), lambda b, i: (b, kcol)),
                  pl.BlockSpec((N_MEM, X_WIDTH), lambda b, i: (b, vcol))],
        out_specs=pl.BlockSpec((tq, X_WIDTH), lambda b, i: (b * per + i, 0)),
        out_shape=jax.ShapeDtypeStruct((n, X_WIDTH), F32),
        compiler_params=_params("parallel", "parallel"),
        name="cross",
    )(qc, mk, mv)


def _top_values(s, k):
    vals = []
    for _ in range(k):
        m = jnp.max(s, axis=0, keepdims=True)
        vals.append(m)
        s = jnp.where(s == m, NEG_INF, s)
    return jnp.concatenate(vals, axis=0)


def _merge_sort_pairs(n):
    pairs = []
    p = 1
    while p < n:
        k = p
        while k >= 1:
            for j in range(k % p, n - k, 2 * k):
                for i in range(min(k, n - j - k)):
                    if (i + j) // (2 * p) == (i + j + k) // (2 * p):
                        pairs.append((i + j, i + j + k))
            k //= 2
        p *= 2
    return pairs


def _sort_desc(xs, pairs):
    xs = list(xs)
    for i, j in pairs:
        xs[i], xs[j] = jnp.maximum(xs[i], xs[j]), jnp.minimum(xs[i], xs[j])
    return xs


def _top16_sorted(s):
    n = PEER_TOPK
    sub = s.shape[0] // n
    xs = _sort_desc([s[r * sub:(r + 1) * sub] for r in range(n)], _merge_sort_pairs(n))
    bitonic = [(i, i + d) for d in (8, 4, 2, 1) for i in range(n) if (i // d) % 2 == 0]
    for shift in (4, 2, 1):
        ys = [pltpu.roll(x, shift, axis=0) for x in xs]
        xs = _sort_desc([jnp.maximum(xs[i], ys[n - 1 - i]) for i in range(n)], bitonic)
    return xs


def _rows(slabs):
    row = lax.broadcasted_iota(jnp.int32, slabs[0].shape, 0)
    out = slabs[-1]
    for i in range(len(slabs) - 2, -1, -1):
        out = jnp.where(row == i, slabs[i], out)
    return out


def _pair_sums(x1, v1_hi, x2_0, v2_lo, v2_hi):
    half = PEER_TOPK // 2
    row = lax.broadcasted_iota(jnp.int32, (half, 1), 0)
    parts = [x1[0] + v2_lo, x1[0] + v2_hi]
    for a in range(1, half):
        parts.append(jnp.where(row < PEER_TOPK // (a + 1), x1[a] + v2_lo, NEG_INF))
    parts.append(v1_hi + x2_0)
    return jnp.concatenate(parts, axis=0)


def _peer_pre_kernel(x1_ref, oc_ref, wco_ref, nf_ref, wpq_ref, k1_ref, k2_ref,
                     x2_ref, ht_ref, s1_ref, s2_ref, tau_ref):
    x2 = x1_ref[...] + _dot(oc_ref[...].astype(BF16), wco_ref[...])
    x2_ref[...] = x2
    ht = _rms(x2, nf_ref[...]).T.astype(BF16)
    ht_ref[...] = ht
    qt = _dot(wpq_ref[...], ht)
    log2e = float(np.log2(np.e))

    def scores(keys3, q):
        q_hi = q.astype(BF16)
        q_lo = (q - q_hi.astype(F32)).astype(BF16)
        return _dot(keys3, jnp.concatenate([q_hi, q_lo, q_hi], axis=0)) * log2e

    for h in range(PEER_HEADS):
        r0 = h * 2 * PEER_HALF
        s1 = scores(k1_ref[h], qt[r0:r0 + PEER_HALF])
        s2 = scores(k2_ref[h], qt[r0 + PEER_HALF:r0 + 2 * PEER_HALF])
        x1 = _top16_sorted(s1)
        x2 = _top16_sorted(s2)
        half = PEER_TOPK // 2
        v1_hi, v2_lo, v2_hi = _rows(x1[half:]), _rows(x2[:half]), _rows(x2[half:])
        cand = _pair_sums(x1, v1_hi, x2[0], v2_lo, v2_hi)
        sc = _top_values(cand, PEER_TOPK)
        top = sc[0:1]
        shift = top + jnp.log2(jnp.sum(jnp.exp2(sc - top), axis=0, keepdims=True))
        picked = cand >= sc[PEER_TOPK - 1:PEER_TOPK]
        shifted = _pair_sums(x1, v1_hi, x2[0] - shift, v2_lo - shift, v2_hi - shift)
        tau = jnp.min(jnp.where(picked, shifted, jnp.inf), axis=0, keepdims=True)
        s1_ref[h] = s1
        s2_ref[h] = s2 - shift
        tau_ref[h:h + 1, :] = tau


def peer_pre(x1, oc, wco, nf, wpq_t, k1, k2, tb):
    n = x1.shape[0]
    const = lambda a: pl.BlockSpec(a.shape, lambda i: (0,) * a.ndim)
    args = (wco, nf.reshape(1, D_MODEL), wpq_t, k1, k2)
    return pl.pallas_call(
        _peer_pre_kernel,
        grid=(n // tb,),
        in_specs=[pl.BlockSpec((tb, D_MODEL), lambda i: (i, 0)), pl.BlockSpec((tb, X_WIDTH), lambda i: (i, 0))]
        + [const(a) for a in args],
        out_specs=[pl.BlockSpec((tb, D_MODEL), lambda i: (i, 0)),
                   pl.BlockSpec((D_MODEL, tb), lambda i: (0, i)),
                   pl.BlockSpec((PEER_HEADS, PEER_NKEYS, tb), lambda i: (0, 0, i)),
                   pl.BlockSpec((PEER_HEADS, PEER_NKEYS, tb), lambda i: (0, 0, i)),
                   pl.BlockSpec((PEER_HEADS, tb), lambda i: (0, i))],
        out_shape=[jax.ShapeDtypeStruct((n, D_MODEL), F32),
                   jax.ShapeDtypeStruct((D_MODEL, n), BF16),
                   jax.ShapeDtypeStruct((PEER_HEADS, PEER_NKEYS, n), F32),
                   jax.ShapeDtypeStruct((PEER_HEADS, PEER_NKEYS, n), F32),
                   jax.ShapeDtypeStruct((PEER_HEADS, n), F32)],
        compiler_params=_params("parallel"),
        name="peer_pre",
    )(x1, oc, *args)


PEER_ROWS = 16
PEER_EBLOCK = 1024
PEER_ECHUNK = 512
PEER_TCHUNK = 256


def _peer_dense_kernel(ht_ref, s1_ref, s2_ref, tau_ref, u_ref, vt_ref, x2_ref, g_ref,
                       y_ref, acc_ref, *pieces, eb, tb):
    j = pl.program_id(1)

    @pl.when(j == 0)
    def _():
        acc_ref[...] = jnp.zeros_like(acc_ref)

    ec, tcw = PEER_ECHUNK, min(PEER_TCHUNK, tb)
    n_ec, n_tc = eb // ec, tb // tcw
    at_refs = [pieces[me * n_tc:(me + 1) * n_tc] for me in range(n_ec)]
    ct_refs = pieces[n_ec * n_tc:]

    def pre_act(me, nt):
        at_refs[me][nt][...] = _dot(u_ref[me * ec:(me + 1) * ec, :], ht_ref[:, nt * tcw:(nt + 1) * tcw])

    def strip(me, nt):
        n_loc = ec // PEER_NKEYS
        for tc in range(tcw // LANES):
            ls = slice(tc * LANES, (tc + 1) * LANES)
            gs = slice(nt * tcw + tc * LANES, nt * tcw + (tc + 1) * LANES)
            for rb in range(PEER_NKEYS // PEER_ROWS):
                rs = slice(rb * PEER_ROWS, (rb + 1) * PEER_ROWS)
                w = [jnp.zeros((PEER_ROWS, LANES), F32) for _ in range(n_loc)]
                for h in range(PEER_HEADS):
                    s2 = s2_ref[h, rs, gs]
                    tau = tau_ref[h:h + 1, gs]
                    for il in range(n_loc):
                        ii = me * n_loc + il
                        sm = s2 + s1_ref[h, ii:ii + 1, gs]
                        w[il] = w[il] + jnp.where(sm >= tau, jnp.exp2(sm), 0.0)
                for il in range(n_loc):
                    r0 = il * PEER_NKEYS + rb * PEER_ROWS
                    a = at_refs[me][nt][r0:r0 + PEER_ROWS, ls]
                    gelu = 0.5 * a * (1.0 + lax.erf(a * (2.0 ** -0.5)))
                    ct_refs[nt][me * ec + r0:me * ec + r0 + PEER_ROWS, ls] = (w[il] * gelu).astype(BF16)

    def project(mo, nt):
        rows = slice(mo * (D_MODEL // n_ec), (mo + 1) * (D_MODEL // n_ec))
        acc_ref[rows, nt * tcw:(nt + 1) * tcw] += _dot(vt_ref[0, rows, :], ct_refs[nt][...])

    order = [(me, nt) for nt in range(n_tc) for me in range(n_ec)]
    pre_act(*order[0])
    for k, (me, nt) in enumerate(order):
        if k + 1 < len(order):
            pre_act(*order[k + 1])
        if nt > 0:
            project(me, nt - 1)
        strip(me, nt)
    for mo in range(n_ec):
        project(mo, n_tc - 1)

    @pl.when(j == pl.num_programs(1) - 1)
    def _():
        y_ref[...] = _rms(x2_ref[...] + acc_ref[...].T, g_ref[...])


def peer_dense(ht, s1, s2, tau, u, vt, x2, g, tb, eb):
    n = x2.shape[0]
    tok = lambda rows: pl.BlockSpec((rows, tb), lambda i, j: (0, i))
    tcw = min(PEER_TCHUNK, tb)
    pieces = ([pltpu.VMEM((PEER_ECHUNK, tcw), F32)] * ((eb // PEER_ECHUNK) * (tb // tcw))
              + [pltpu.VMEM((eb, tcw), BF16)] * (tb // tcw))
    return pl.pallas_call(
        functools.partial(_peer_dense_kernel, eb=eb, tb=tb),
        grid=(n // tb, PEER_N // eb),
        in_specs=[tok(D_MODEL), pl.BlockSpec((PEER_HEADS, eb // PEER_NKEYS, tb), lambda i, j: (0, j, i)),
                  pl.BlockSpec((PEER_HEADS, PEER_NKEYS, tb), lambda i, j: (0, 0, i)), tok(PEER_HEADS),
                  pl.BlockSpec((eb, D_MODEL), lambda i, j: (j, 0)),
                  pl.BlockSpec((1, D_MODEL, eb), lambda i, j: (j, 0, 0)),
                  pl.BlockSpec((tb, D_MODEL), lambda i, j: (i, 0)),
                  pl.BlockSpec((1, D_MODEL), lambda i, j: (0, 0))],
        out_specs=pl.BlockSpec((tb, D_MODEL), lambda i, j: (i, 0)),
        out_shape=jax.ShapeDtypeStruct((n, D_MODEL), F32),
        scratch_shapes=[pltpu.VMEM((D_MODEL, tb), F32)] + pieces,
        compiler_params=_params("parallel", "arbitrary"),
        name="peer_dense",
    )(ht, s1, s2, tau, u, vt, x2, g.reshape(1, D_MODEL))


def _split3(k):
    hi = k.astype(BF16)
    lo = (k - hi.astype(F32)).astype(BF16)
    return jnp.concatenate([hi, hi, lo], axis=-1)


def _trunk(x, z, oa, ob, mk, mv, kcol, vcol, batch, tq, w, tb_tok, tb_peer, pad_rows):
    x1, qc = mix_out(x, oa, ob, z, w["beta_a"], w["gnorm_b"], w["w_out"], w["norm_cross"], w["w_cq"], tb_tok)
    if pad_rows:
        n_new = x.shape[0] // batch
        qc = jnp.pad(qc.reshape(batch, n_new, X_WIDTH), ((0, 0), (0, pad_rows - n_new), (0, 0)))
        oc = cross(qc.reshape(batch * pad_rows, X_WIDTH), mk, mv, kcol, vcol, batch, tq)
        oc = oc.reshape(batch, pad_rows, X_WIDTH)[:, :n_new].reshape(x.shape[0], X_WIDTH)
    else:
        oc = cross(qc, mk, mv, kcol, vcol, batch, tq)
    x2, ht, s1, s2, tau = peer_pre(x1, oc, w["w_co"], w["norm_ffn"], w["w_pq_t"], w["k1"], w["k2"], tb_tok)
    return peer_dense(ht, s1, s2, tau, w["u"], w["v_blocks"], x2, w["norm_final"], tb_peer, PEER_EBLOCK)


def kernel(x_prompt, x_sample, cache_swa_k, cache_swa_v, state_hgrn, cache_mem_k, cache_mem_v, mem_prompt, norm_mix, w_in, lb_logits, beta_a, gnorm_b, w_out, norm_cross, norm_mem, w_cq, w_mk, w_mv, w_co, norm_ffn, w_pq, peer_k1, peer_k2, peer_u, peer_v, norm_final):
    batch, seq, _ = x_prompt.shape
    dbatch, dseq, _ = x_sample.shape
    win = cache_swa_k.shape[2]
    rows8 = 8
    layer = 0
    lb = jnp.cumsum(jax.nn.softmax(lb_logits.astype(F32), axis=0), axis=0)[layer].reshape(1, B_WIDTH)
    w = {
        "beta_a": beta_a[layer], "gnorm_b": gnorm_b[layer], "w_out": w_out[layer].astype(BF16),
        "norm_cross": norm_cross[layer], "w_cq": w_cq[layer].astype(BF16), "w_co": w_co[layer].astype(BF16),
        "norm_ffn": norm_ffn[layer], "w_pq_t": w_pq[layer].T.astype(BF16),
        "k1": _split3(peer_k1[layer]), "k2": _split3(peer_k2[layer]),
        "u": peer_u[layer].astype(BF16), "norm_final": norm_final,
        "v_blocks": peer_v[layer].astype(BF16).reshape(PEER_N // PEER_EBLOCK, PEER_EBLOCK, D_MODEL).transpose(0, 2, 1),
    }
    w_in_b = w_in[layer].astype(BF16)
    w_mem = jnp.concatenate([w_mk[layer], w_mv[layer]], axis=1).astype(BF16)

    xp = x_prompt.reshape(batch * seq, D_MODEL)
    zp = norm_matmul(xp, norm_mix[layer], w_in_b, 512)
    oa_p = attn_prompt(zp, batch, seq)
    ob_p, st_p = hgrn(zp, lb, None, batch, seq, 64, HGRN_GROUP, 16, 64)
    memkv = norm_matmul(mem_prompt.reshape(batch * N_MEM, D_MODEL), norm_mem[layer], w_mem, 256)
    y_p = _trunk(xp, zp, oa_p, ob_p, memkv, memkv, 0, 1, batch, 512, w, 512, 1024, 0)

    xs = x_sample.reshape(dbatch * dseq, D_MODEL)
    zs = norm_matmul(xs, norm_mix[layer], w_in_b, dbatch * dseq)
    zs8 = jnp.pad(zs.reshape(dbatch, dseq, MIX_IN), ((0, 0), (0, rows8 - dseq), (0, 0)))
    feature_major = lambda c: jnp.transpose(c, (0, 2, 3, 1)).reshape(dbatch, A_WIDTH, win)
    oa_s = attn_sample(zs8, feature_major(cache_swa_k[layer]), feature_major(cache_swa_v[layer]), dseq)
    oa_s = oa_s[:, :dseq].reshape(dbatch * dseq, A_WIDTH)
    ob_s, st_s = hgrn(zs8.reshape(dbatch * rows8, MIX_IN), lb, state_hgrn[layer], dbatch, rows8, rows8, 1, rows8, dseq)
    ob_s = ob_s.reshape(dbatch, rows8, B_WIDTH)[:, :dseq].reshape(dbatch * dseq, B_WIDTH)
    y_s = _trunk(xs, zs, oa_s, ob_s, cache_mem_k[layer].reshape(dbatch * N_MEM, X_WIDTH),
                 cache_mem_v[layer].reshape(dbatch * N_MEM, X_WIDTH), 0, 0, dbatch, rows8, w,
                 dbatch * dseq, dbatch * dseq, rows8)

    keep = min(BRANCHES[-1][0], seq)
    kv_t = window_rows(zp, batch, seq, keep)
    kp, vp = (jnp.transpose(kv_t[i].reshape(batch, A_HEADS, A_HD, keep), (0, 3, 1, 2))[None] for i in range(2))
    return (y_p.reshape(batch, seq, D_MODEL), y_s.reshape(dbatch, dseq, D_MODEL),
            kp, vp, st_p[None],
            memkv[:, :X_WIDTH].reshape(1, batch, N_MEM, X_HEADS, X_HD),
            memkv[:, X_WIDTH:].reshape(1, batch, N_MEM, X_HEADS, X_HD),
            zs[:, A_WIDTH:2 * A_WIDTH].reshape(1, dbatch, dseq, A_HEADS, A_HD),
            zs[:, 2 * A_WIDTH:3 * A_WIDTH].reshape(1, dbatch, dseq, A_HEADS, A_HD),
            st_s[None])
```
